```python
import jax, jax.numpy as jnp
from jax import lax
import numpy as np

D_MODEL = 1024
BATCH = 4
SEQ = 8192
DEPTH = 2

MEM_LEN = 256
EPS = 1e-6
MIX_WIDTH = D_MODEL
ML_HEADS = 4
ML_WIDTH = MIX_WIDTH // 2
ML_HD = ML_WIDTH // ML_HEADS
ML_CHUNK = 64
RG_WIDTH = MIX_WIDTH - ML_WIDTH
RG_BLOCKS = 4
RG_BW = RG_WIDTH // RG_BLOCKS
RG_CONV = 4
RG_C = 8.0
IN_COLS = 4 * ML_WIDTH + 2 * ML_HEADS + 2 * RG_WIDTH
SW_HEADS = 16
SW_KV_HEADS = 4
SW_HD = D_MODEL // SW_HEADS
SW_GROUP = SW_HEADS // SW_KV_HEADS
SW_WINDOW = 128
SW_BLOCK = 128
QKV_COLS = (SW_HEADS + 2 * SW_KV_HEADS) * SW_HD
ROPE_THETA = 10000.0
X_HEADS = 4
X_HD = D_MODEL // X_HEADS
D_FF = 4 * D_MODEL
N_EVEN = (DEPTH + 1) // 2
N_ODD = DEPTH // 2

kernel_name = 'hybrid_mlstm_rglru_swa_sink_block'


def rms_norm(x, g):
    xf = x.astype(jnp.float32)
    y = xf * lax.rsqrt(jnp.mean(xf * xf, axis=-1, keepdims=True) + EPS)
    return (y * g.astype(jnp.float32)).astype(x.dtype)


def rotary(t, positions):
    d = t.shape[-1]
    half = d // 2
    inv = jnp.power(ROPE_THETA, -jnp.arange(half, dtype=jnp.float32) * 2.0 / d)
    ang = positions.astype(jnp.float32)[..., None] * inv
    cos = jnp.cos(ang)[:, :, None, :]
    sin = jnp.sin(ang)[:, :, None, :]
    t1 = t[..., :half].astype(jnp.float32)
    t2 = t[..., half:].astype(jnp.float32)
    return jnp.concatenate([t1 * cos - t2 * sin, t2 * cos + t1 * sin], axis=-1).astype(t.dtype)


def mlstm_chunkwise(q, k, v, i_pre, f_pre):
    B, S, H, dh = q.shape
    nc = S // ML_CHUNK
    f32 = jnp.float32

    def to_chunks(t):
        t = t.astype(f32).reshape((B, nc, ML_CHUNK) + t.shape[2:])
        return jnp.moveaxis(t, (1, 3), (0, 2))

    qc = to_chunks(q)
    kc = to_chunks(k) * (dh ** -0.5)
    vc = to_chunks(v)
    logi = to_chunks(i_pre)
    logf = to_chunks(jax.nn.log_sigmoid(f_pre.astype(f32)))
    causal = jnp.tril(jnp.ones((ML_CHUNK, ML_CHUNK), dtype=bool))

    def step(carry, inp):
        C, n, m = carry
        qb, kb, vb, li, lf = inp
        b = jnp.cumsum(lf, axis=-1)
        dmat = b[..., :, None] - b[..., None, :] + li[..., None, :]
        dmat = jnp.where(causal, dmat, -jnp.inf)
        m_t = jnp.maximum(b + m[..., None], jnp.max(dmat, axis=-1))
        decay_prev = jnp.exp(b + m[..., None] - m_t)
        s = jnp.einsum('bhtd,bhsd->bhts', qb, kb) * jnp.exp(dmat - m_t[..., None])
        num = jnp.einsum('bhts,bhsd->bhtd', s, vb) + decay_prev[..., None] * jnp.einsum('bhtd,bhde->bhte', qb, C)
        den = jnp.sum(s, axis=-1) + decay_prev * jnp.einsum('bhtd,bhd->bht', qb, n)
        h = num / jnp.maximum(jnp.abs(den), jnp.exp(-m_t))[..., None]
        b_last = b[..., -1]
        g = b_last[..., None] - b + li
        m_new = jnp.maximum(b_last + m, jnp.max(g, axis=-1))
        w_prev = jnp.exp(b_last + m - m_new)
        w_s = jnp.exp(g - m_new[..., None])
        C_new = w_prev[..., None, None] * C + jnp.einsum('bhs,bhsd,bhse->bhde', w_s, kb, vb)
        n_new = w_prev[..., None] * n + jnp.einsum('bhs,bhsd->bhd', w_s, kb)
        return (C_new, n_new, m_new), h

    init = (jnp.zeros((B, H, dh, dh), f32), jnp.zeros((B, H, dh), f32), jnp.zeros((B, H), f32))
    _, hs = lax.scan(step, init, (qc, kc, vc, logi, logf))
    hs = jnp.moveaxis(hs, (0, 2), (1, 3))
    return hs.reshape(B, S, H * dh)


def rg_lru_branch(rx, conv_w, conv_b, rg_wa, rg_ba, rg_wx, rg_bx, rg_lambda):
    B, S, _ = rx.shape
    xc = lax.conv_general_dilated(rx, conv_w[:, None, :], window_strides=(1,),
                                  padding=[(RG_CONV - 1, 0)],
                                  dimension_numbers=('NWC', 'WIO', 'NWC'),
                                  feature_group_count=RG_WIDTH) + conv_b
    xb = xc.reshape(B, S, RG_BLOCKS, RG_BW)
    r = jax.nn.sigmoid((jnp.einsum('bsnc,ncd->bsnd', xb, rg_wa).reshape(B, S, RG_WIDTH) + rg_ba).astype(jnp.float32))
    i = jax.nn.sigmoid((jnp.einsum('bsnc,ncd->bsnd', xb, rg_wx).reshape(B, S, RG_WIDTH) + rg_bx).astype(jnp.float32))
    log_a = -RG_C * r * jax.nn.softplus(-rg_lambda.astype(jnp.float32))
    a = jnp.exp(log_a)
    u = jnp.sqrt(-jnp.expm1(2.0 * log_a)) * (i * xc.astype(jnp.float32))

    def combine(e1, e2):
        a1, b1 = e1
        a2, b2 = e2
        return a1 * a2, a2 * b1 + b2

    _, h = lax.associative_scan(combine, (a, u), axis=1)
    return h


def mlstm_rglru_mixer(u, w_in, b_if, conv_w, conv_b, rg_wa, rg_ba, rg_wx, rg_bx, rg_lambda, w_out):
    B, S, _ = u.shape
    z = u @ w_in
    q, k, v, o, gif, rx, ry = jnp.split(z, [ML_WIDTH, 2 * ML_WIDTH, 3 * ML_WIDTH, 4 * ML_WIDTH,
                                            4 * ML_WIDTH + 2 * ML_HEADS,
                                            4 * ML_WIDTH + 2 * ML_HEADS + RG_WIDTH], axis=-1)
    gif = gif + b_if
    i_pre, f_pre = gif[..., :ML_HEADS], gif[..., ML_HEADS:]
    hm = mlstm_chunkwise(q.reshape(B, S, ML_HEADS, ML_HD), k.reshape(B, S, ML_HEADS, ML_HD),
                         v.reshape(B, S, ML_HEADS, ML_HD), i_pre, f_pre)
    y_a = jax.nn.sigmoid(o) * hm.astype(u.dtype)
    h_rg = rg_lru_branch(rx, conv_w, conv_b, rg_wa, rg_ba, rg_wx, rg_bx, rg_lambda)
    y_b = jax.nn.gelu(ry) * h_rg.astype(u.dtype)
    return jnp.concatenate([y_a, y_b], axis=-1) @ w_out


def sliding_window_sink_attention(u, positions, w_qkv, sinks, w_o):
    B, S, _ = u.shape
    qkv = u @ w_qkv
    q, k, v = jnp.split(qkv, [SW_HEADS * SW_HD, (SW_HEADS + SW_KV_HEADS) * SW_HD], axis=-1)
    q = rotary(q.reshape(B, S, SW_HEADS, SW_HD), positions).reshape(B, S, SW_KV_HEADS, SW_GROUP, SW_HD)
    k = rotary(k.reshape(B, S, SW_KV_HEADS, SW_HD), positions)
    v = v.reshape(B, S, SW_KV_HEADS, SW_HD)
    pad = ((0, 0), (SW_BLOCK, 0), (0, 0), (0, 0))
    kp = jnp.pad(k, pad)
    vp = jnp.pad(v, pad)
    nb = S // SW_BLOCK
    r = jnp.arange(SW_BLOCK)[:, None]
    c = jnp.arange(2 * SW_BLOCK)[None, :]
    diff = r + SW_BLOCK - c
    band = (diff >= 0) & (diff < SW_WINDOW)
    sink = sinks.astype(jnp.float32).reshape(SW_KV_HEADS, SW_GROUP)[None, :, :, None]
    scale = SW_HD ** -0.5

    def block(j):
        start = j * SW_BLOCK
        qb = lax.dynamic_slice_in_dim(q, start, SW_BLOCK, axis=1)
        kb = lax.dynamic_slice_in_dim(kp, start, 2 * SW_BLOCK, axis=1)
        vb = lax.dynamic_slice_in_dim(vp, start, 2 * SW_BLOCK, axis=1)
        s = jnp.einsum('blkgd,bckd->bkglc', qb, kb).astype(jnp.float32) * scale
        valid = band & (start + c - SW_BLOCK >= 0)
        s = jnp.where(valid, s, -jnp.inf)
        m = jnp.maximum(jnp.max(s, axis=-1), sink)
        p = jnp.exp(s - m[..., None])
        den = jnp.sum(p, axis=-1) + jnp.exp(sink - m)
        ob = jnp.einsum('bkglc,bckd->blkgd', p, vb.astype(jnp.float32))
        ob = ob / jnp.moveaxis(den, -1, 1)[..., None]
        return ob.astype(u.dtype)

    out = lax.map(block, jnp.arange(nb))
    out = jnp.moveaxis(out, 0, 1).reshape(B, S, SW_HEADS * SW_HD)
    return out @ w_o


def memory_cross_attention(u, mem, g_mem, w_q, w_kv, w_o):
    B, S, _ = u.shape
    M = mem.shape[1]
    q = (u @ w_q).reshape(B, S, X_HEADS, X_HD)
    kv = (rms_norm(mem, g_mem) @ w_kv).reshape(B, M, 2, X_HEADS, X_HD)
    k, v = kv[:, :, 0], kv[:, :, 1]
    s = jnp.einsum('bshd,bmhd->bhsm', q, k).astype(jnp.float32) * (X_HD ** -0.5)
    p = jax.nn.softmax(s, axis=-1)
    o = jnp.einsum('bhsm,bmhd->bshd', p.astype(v.dtype), v).reshape(B, S, D_MODEL)
    return o @ w_o


def squared_relu_mlp(u, w1, w2):
    return jnp.square(jax.nn.relu(u @ w1)) @ w2


def setup_inputs(seed: int = 0) -> dict:
    key = jax.random.key(seed)
    ks = iter(jax.random.split(key, 40))

    def nrm(shape, scale):
        return jax.random.normal(next(ks), shape, jnp.float32) * scale

    def gains(n):
        return 1.0 + nrm((n, D_MODEL), 0.02)

    x = nrm((BATCH, SEQ, D_MODEL), 1.0)
    mem = nrm((BATCH, MEM_LEN, D_MODEL), 1.0)
    positions = jnp.broadcast_to(jnp.arange(SEQ, dtype=jnp.int32)[None, :], (BATCH, SEQ))
    ev_w_in = nrm((N_EVEN, D_MODEL, IN_COLS), D_MODEL ** -0.5)
    ev_b_if = jnp.concatenate([nrm((N_EVEN, ML_HEADS), 0.1),
                               jnp.linspace(3.0, 6.0, ML_HEADS)[None, :] + nrm((N_EVEN, ML_HEADS), 0.01)], axis=-1)
    ev_conv_w = nrm((N_EVEN, RG_CONV, RG_WIDTH), RG_CONV ** -0.5)
    ev_conv_b = nrm((N_EVEN, RG_WIDTH), 0.01)
    ev_rg_wa = nrm((N_EVEN, RG_BLOCKS, RG_BW, RG_BW), RG_BW ** -0.5)
    ev_rg_ba = nrm((N_EVEN, RG_WIDTH), 0.01)
    ev_rg_wx = nrm((N_EVEN, RG_BLOCKS, RG_BW, RG_BW), RG_BW ** -0.5)
    ev_rg_bx = nrm((N_EVEN, RG_WIDTH), 0.01)
    a_c = jax.random.uniform(next(ks), (N_EVEN, RG_WIDTH), jnp.float32, 0.9, 0.999)
    a_base = a_c ** (1.0 / RG_C)
    ev_rg_lambda = jnp.log(a_base) - jnp.log1p(-a_base)
    ev_w_out = nrm((N_EVEN, MIX_WIDTH, D_MODEL), MIX_WIDTH ** -0.5)
    od_w_qkv = nrm((N_ODD, D_MODEL, QKV_COLS), D_MODEL ** -0.5)
    od_sinks = nrm((N_ODD, SW_HEADS), 1.0)
    od_w_o = nrm((N_ODD, SW_HEADS * SW_HD, D_MODEL), (SW_HEADS * SW_HD) ** -0.5)
    return {
        'x': x, 'mem': mem, 'positions': positions,
        'ev_w_in': ev_w_in, 'ev_b_if': ev_b_if, 'ev_conv_w': ev_conv_w, 'ev_conv_b': ev_conv_b,
        'ev_rg_wa': ev_rg_wa, 'ev_rg_ba': ev_rg_ba, 'ev_rg_wx': ev_rg_wx, 'ev_rg_bx': ev_rg_bx,
        'ev_rg_lambda': ev_rg_lambda, 'ev_w_out': ev_w_out,
        'od_w_qkv': od_w_qkv, 'od_sinks': od_sinks, 'od_w_o': od_w_o,
        'g_mix_pre': gains(DEPTH), 'g_mix_post': gains(DEPTH),
        'g_x_pre': gains(DEPTH), 'g_x_post': gains(DEPTH), 'g_mem': gains(DEPTH),
        'w_xq': nrm((DEPTH, D_MODEL, D_MODEL), D_MODEL ** -0.5),
        'w_xkv': nrm((DEPTH, D_MODEL, 2 * D_MODEL), D_MODEL ** -0.5),
        'w_xo': nrm((DEPTH, D_MODEL, D_MODEL), D_MODEL ** -0.5),
        'g_ff_pre': gains(DEPTH), 'g_ff_post': gains(DEPTH),
        'w_ff1': nrm((DEPTH, D_MODEL, D_FF), D_MODEL ** -0.5),
        'w_ff2': nrm((DEPTH, D_FF, D_MODEL), D_FF ** -0.5),
    }


def reference(x, mem, positions, ev_w_in, ev_b_if, ev_conv_w, ev_conv_b, ev_rg_wa, ev_rg_ba,
              ev_rg_wx, ev_rg_bx, ev_rg_lambda, ev_w_out, od_w_qkv, od_sinks, od_w_o,
              g_mix_pre, g_mix_post, g_x_pre, g_x_post, g_mem, w_xq, w_xkv, w_xo,
              g_ff_pre, g_ff_post, w_ff1, w_ff2):
    h = x
    for l in range(DEPTH):
        u = rms_norm(h, g_mix_pre[l])
        if l % 2 == 0:
            e = l // 2
            u = mlstm_rglru_mixer(u, ev_w_in[e], ev_b_if[e], ev_conv_w[e], ev_conv_b[e],
                                  ev_rg_wa[e], ev_rg_ba[e], ev_rg_wx[e], ev_rg_bx[e],
                                  ev_rg_lambda[e], ev_w_out[e])
        else:
            o = l // 2
            u = sliding_window_sink_attention(u, positions, od_w_qkv[o], od_sinks[o], od_w_o[o])
        h = h + rms_norm(u, g_mix_post[l])
        u = memory_cross_attention(rms_norm(h, g_x_pre[l]), mem, g_mem[l], w_xq[l], w_xkv[l], w_xo[l])
        h = h + rms_norm(u, g_x_post[l])
        u = squared_relu_mlp(rms_norm(h, g_ff_pre[l]), w_ff1[l], w_ff2[l])
        h = h + rms_norm(u, g_ff_post[l])
    return h
```

```python
import functools
import math

import jax
import jax.numpy as jnp
from jax import lax
from jax.experimental import pallas as pl
from jax.experimental.pallas import tpu as pltpu

EPS = 1e-6
MXU_DTYPE = jnp.bfloat16
F32 = jnp.float32
LANES = 128
SUBLANES = 8
VMEM_LIMIT_BYTES = 56 * 1024 * 1024
NEG_BIG = -1e30

ML_HEADS = 4
ML_HD = 128
ML_CHUNK = 128
RG_BLOCKS = 4
RG_BW = 128
RG_CONV = 4
RG_C = 8.0
SW_HEADS = 16
SW_KV_HEADS = 4
SW_HD = 64
SW_GROUP = SW_HEADS // SW_KV_HEADS
SW_BLOCK = 128
ROPE_THETA = 10000.0
X_HEADS = 4
FF_CHUNK = 1024
ROW_TILE = 512
GATE_ROWS = 16


def _mm(a, b):
    return jnp.dot(a.astype(MXU_DTYPE), b.astype(MXU_DTYPE), preferred_element_type=F32)


def _mm_nt(a, b):
    return lax.dot_general(a.astype(MXU_DTYPE), b.astype(MXU_DTYPE),
                           (((1,), (1,)), ((), ())), preferred_element_type=F32)


def _rms(x, g):
    ms = jnp.mean(x * x, axis=-1, keepdims=True)
    return x * lax.rsqrt(ms + EPS) * g


def _softplus(x):
    return jnp.maximum(x, 0.0) + jnp.log1p(jnp.exp(-jnp.abs(x)))


def _log_sigmoid(x):
    return -_softplus(-x)


def _split3(x):
    p1 = x.astype(jnp.bfloat16)
    r1 = x - p1.astype(F32)
    p2 = r1.astype(jnp.bfloat16)
    r2 = r1 - p2.astype(F32)
    return p1, p2, r2.astype(jnp.bfloat16)


def _const_spec(shape):
    nd = len(shape)
    return pl.BlockSpec(shape, lambda *_: (0,) * nd, pipeline_mode=pl.Buffered(1))


def _params(*sem):
    return pltpu.CompilerParams(dimension_semantics=sem, vmem_limit_bytes=VMEM_LIMIT_BYTES)


def _inproj_kernel(h_ref, g_ref, wa_ref, wb_ref, wg_ref, wgt_ref, bc_ref, br_ref,
                   za_ref, zb_ref, gc_ref, gr_ref):
    u = _rms(h_ref[...], g_ref[...]).astype(MXU_DTYPE)
    za_ref[...] = jnp.dot(u, wa_ref[...], preferred_element_type=F32)
    zb_ref[...] = jnp.dot(u, wb_ref[...], preferred_element_type=F32)
    gc = jnp.dot(u, wg_ref[...], preferred_element_type=F32)
    gc_ref[...] = gc[:, :2 * ML_HEADS] + bc_ref[...]
    gr_ref[...] = _mm_nt(wgt_ref[...], u) + br_ref[...]


def _inproj(h, g, wa, wb, wg, wgt, bc, br):
    T, D = h.shape
    tm = ROW_TILE
    na, nb = wa.shape[1], wb.shape[1]
    return pl.pallas_call(
        _inproj_kernel,
        grid=(T // tm,),
        in_specs=[pl.BlockSpec((tm, D), lambda i: (i, 0)),
                  _const_spec((1, D)), _const_spec(wa.shape), _const_spec(wb.shape),
                  _const_spec(wg.shape), _const_spec(wgt.shape),
                  _const_spec(bc.shape), _const_spec(br.shape)],
        out_specs=[pl.BlockSpec((tm, na), lambda i: (i, 0)),
                   pl.BlockSpec((tm, nb), lambda i: (i, 0)),
                   pl.BlockSpec((tm, 2 * ML_HEADS), lambda i: (i, 0)),
                   pl.BlockSpec((GATE_ROWS, tm), lambda i: (0, i))],
        out_shape=[jax.ShapeDtypeStruct((T, na), F32), jax.ShapeDtypeStruct((T, nb), F32),
                   jax.ShapeDtypeStruct((T, 2 * ML_HEADS), F32),
                   jax.ShapeDtypeStruct((GATE_ROWS, T), F32)],
        compiler_params=_params("parallel"),
        name="inproj",
    )(h, g, wa, wb, wg, wgt, bc, br)


def _mlstm_kernel(q_ref, k_ref, v_ref, o_ref, gc_ref, gr_ref, y_ref, c_ref, m_ref):
    L = q_ref.shape[0]
    dh = ML_HD

    @pl.when(pl.program_id(1) == 0)
    def _():
        c_ref[...] = jnp.zeros_like(c_ref)
        m_ref[...] = jnp.zeros_like(m_ref)

    row = lax.broadcasted_iota(jnp.int32, (L, L), 0)
    col = lax.broadcasted_iota(jnp.int32, (L, L), 1)
    causal = col <= row
    tril = causal.astype(MXU_DTYPE)
    triu = (row <= col).astype(MXU_DTYPE)

    gr = gr_ref[...]
    lf_r = _log_sigmoid(gr)
    b_r = sum(jnp.dot(p, triu, preferred_element_type=F32) for p in _split3(lf_r))
    lf_c = _log_sigmoid(gc_ref[...])
    lf_cb = jnp.concatenate(
        [jnp.broadcast_to(lf_c[:, ML_HEADS + h:ML_HEADS + h + 1], (L, LANES)) for h in range(ML_HEADS)], axis=1)
    b_cb = sum(jnp.dot(tril, p, preferred_element_type=F32) for p in _split3(lf_cb))

    li_r = gr[0:ML_HEADS, :]
    bf_r = b_r[ML_HEADS:2 * ML_HEADS, :]
    m_prev = m_ref[...]
    b_last = bf_r[:, L - 1:L]
    g_r = b_last - bf_r + li_r
    m_new = jnp.maximum(b_last + m_prev, jnp.max(g_r, axis=1, keepdims=True))
    w_prev = jnp.exp(b_last + m_prev - m_new)
    w_r = jnp.exp(g_r - m_new) * (dh ** -0.5)
    e_r = li_r - bf_r
    m_ref[...] = m_new

    ones_col = (lax.broadcasted_iota(jnp.int32, (L, dh), 1) == 0).astype(F32)
    q_all = q_ref[...]
    k_all = k_ref[...]
    v_all = v_ref[...]
    o_all = o_ref[...]
    outs = []
    for h in range(ML_HEADS):
        sl = slice(h * dh, (h + 1) * dh)
        q = q_all[:, sl].astype(MXU_DTYPE)
        k = k_all[:, sl]
        v_aug = jnp.concatenate([v_all[:, sl], ones_col], axis=1).astype(MXU_DTYPE)
        c_aug = c_ref[h]
        b_t = b_cb[:, h * LANES:h * LANES + 1]
        mp = m_prev[h:h + 1, :]
        dmat = jnp.where(causal, b_t + e_r[h:h + 1, :], NEG_BIG)
        m_t = jnp.maximum(b_t + mp, jnp.max(dmat, axis=1, keepdims=True))
        s = _mm_nt(q, k) * (jnp.exp(dmat - m_t) * (dh ** -0.5))
        tot = _mm(s, v_aug) + jnp.exp(b_t + mp - m_t) * _mm(q, c_aug)
        den = jnp.maximum(jnp.abs(tot[:, dh:dh + 1]), jnp.exp(-m_t))
        outs.append(jax.nn.sigmoid(o_all[:, sl]) * (tot[:, :dh] / den))
        kw_t = k.T * w_r[h:h + 1, :]
        c_ref[h] = w_prev[h:h + 1, :] * c_aug + _mm(kw_t, v_aug)
    y_ref[...] = jnp.concatenate(outs, axis=1)


def _mlstm(za, gc, gr, B, S):
    T = za.shape[0]
    L = ML_CHUNK
    nc = S // L
    W = ML_HEADS * ML_HD

    def col(j):
        return pl.BlockSpec((L, W), lambda b, c: (b * nc + c, j))

    return pl.pallas_call(
        _mlstm_kernel,
        grid=(B, nc),
        in_specs=[col(0), col(1), col(2), col(3),
                  pl.BlockSpec((L, 2 * ML_HEADS), lambda b, c: (b * nc + c, 0)),
                  pl.BlockSpec((GATE_ROWS, L), lambda b, c: (0, b * nc + c))],
        out_specs=pl.BlockSpec((L, W), lambda b, c: (b * nc + c, 0)),
        out_shape=jax.ShapeDtypeStruct((T, W), F32),
        scratch_shapes=[pltpu.VMEM((ML_HEADS, ML_HD, 2 * ML_HD), F32),
                        pltpu.VMEM((ML_HEADS, 1), F32)],
        compiler_params=_params("parallel", "arbitrary"),
        name="mlstm",
    )(za, za, za, za, gc, gr)


def _neg_expm1(y):
    series = -y * (1.0 + y * (1.0 / 2) * (1.0 + y * (1.0 / 3) * (1.0 + y * (1.0 / 4) * (
        1.0 + y * (1.0 / 5) * (1.0 + y * (1.0 / 6) * (1.0 + y * (1.0 / 7) * (1.0 + y * (1.0 / 8))))))))
    return jnp.where(y > -0.25, series, 1.0 - jnp.exp(y))


def _gelu_tanh(x):
    return 0.5 * x * (1.0 + jnp.tanh(math.sqrt(2.0 / math.pi) * (x + 0.044715 * (x * x * x))))


def _rglru_kernel(rx_ref, ry_ref, cw_ref, cb_ref, wa_ref, ba_ref, wx_ref, bx_ref, lam_ref,
                  y_ref, tail_ref, hc_ref, a_s, u_s):
    ts, W = rx_ref.shape

    @pl.when(pl.program_id(1) == 0)
    def _():
        tail_ref[...] = jnp.zeros_like(tail_ref)
        hc_ref[...] = jnp.zeros_like(hc_ref)

    x = rx_ref[...]
    prev = tail_ref[...]
    row8 = lax.broadcasted_iota(jnp.int32, (SUBLANES, W), 0)
    cw = cw_ref[...]
    xc = x * cw[RG_CONV - 1:RG_CONV, :] + cb_ref[...]
    for d in range(1, RG_CONV):
        xs = pltpu.roll(x, d, 0)
        top = jnp.where(row8 < d, pltpu.roll(prev, d, 0), xs[0:SUBLANES])
        xs = jnp.concatenate([top, xs[SUBLANES:]], axis=0)
        xc = xc + xs * cw[RG_CONV - 1 - d:RG_CONV - d, :]
    tail_ref[...] = x[ts - SUBLANES:ts]

    xcb = xc.astype(MXU_DTYPE)
    ga = jnp.concatenate([jnp.dot(xcb[:, n * RG_BW:(n + 1) * RG_BW], wa_ref[n], preferred_element_type=F32)
                          for n in range(RG_BLOCKS)], axis=1) + ba_ref[...]
    gx = jnp.concatenate([jnp.dot(xcb[:, n * RG_BW:(n + 1) * RG_BW], wx_ref[n], preferred_element_type=F32)
                          for n in range(RG_BLOCKS)], axis=1) + bx_ref[...]
    r = jax.nn.sigmoid(ga)
    i = jax.nn.sigmoid(gx)
    log_a = (-RG_C * _softplus(-lam_ref[...])) * r
    a = jnp.exp(log_a)
    u = jnp.sqrt(_neg_expm1(2.0 * log_a)) * (i * xc)

    rowmod = lax.broadcasted_iota(jnp.int32, (ts, W), 0) & (SUBLANES - 1)
    for d in (1, 2, 4):
        keep = rowmod >= d
        a_sh = jnp.where(keep, pltpu.roll(a, d, 0), 1.0)
        u_sh = jnp.where(keep, pltpu.roll(u, d, 0), 0.0)
        u = a * u_sh + u
        a = a * a_sh
    a_s[...] = a
    u_s[...] = u

    def group(gi, h_in):
        sl = pl.ds(pl.multiple_of(gi * SUBLANES, SUBLANES), SUBLANES)
        hg = a_s[sl, :] * h_in + u_s[sl, :]
        u_s[sl, :] = hg
        return hg[SUBLANES - 1:SUBLANES, :]

    hc_ref[...] = lax.fori_loop(0, ts // SUBLANES, group, hc_ref[...])
    y_ref[...] = _gelu_tanh(ry_ref[...]) * u_s[...]


def _rglru(zb, cw, cb, wa, ba, wx, bx, lam, B, S):
    T = zb.shape[0]
    W = RG_BLOCKS * RG_BW
    ts = ROW_TILE
    nt = S // ts
    return pl.pallas_call(
        _rglru_kernel,
        grid=(B, nt),
        in_specs=[pl.BlockSpec((ts, W), lambda b, j: (b * nt + j, 0)),
                  pl.BlockSpec((ts, W), lambda b, j: (b * nt + j, 1)),
                  _const_spec(cw.shape), _const_spec(cb.shape), _const_spec(wa.shape), _const_spec(ba.shape),
                  _const_spec(wx.shape), _const_spec(bx.shape), _const_spec(lam.shape)],
        out_specs=pl.BlockSpec((ts, W), lambda b, j: (b * nt + j, 0)),
        out_shape=jax.ShapeDtypeStruct((T, W), F32),
        scratch_shapes=[pltpu.VMEM((SUBLANES, W), F32), pltpu.VMEM((1, W), F32),
                        pltpu.VMEM((ts, W), F32), pltpu.VMEM((ts, W), F32)],
        compiler_params=_params("parallel", "arbitrary"),
        name="rglru",
    )(zb, zb, cw, cb, wa, ba, wx, bx, lam)


def _proj_res_kernel(*refs, n):
    ys, ws = refs[:n], refs[n:2 * n]
    h_ref, g_ref, o_ref = refs[2 * n:]
    acc = _mm(ys[0][...], ws[0][...])
    for y, w in zip(ys[1:], ws[1:]):
        acc = acc + _mm(y[...], w[...])
    o_ref[...] = h_ref[...] + _rms(acc, g_ref[...])


def _proj_res(ys, ws, h, g):
    T, D = h.shape
    tm = ROW_TILE
    n = len(ys)
    return pl.pallas_call(
        functools.partial(_proj_res_kernel, n=n),
        grid=(T // tm,),
        in_specs=([pl.BlockSpec((tm, y.shape[1]), lambda i: (i, 0)) for y in ys]
                  + [_const_spec(w.shape) for w in ws]
                  + [pl.BlockSpec((tm, D), lambda i: (i, 0)), _const_spec((1, D))]),
        out_specs=pl.BlockSpec((tm, D), lambda i: (i, 0)),
        out_shape=jax.ShapeDtypeStruct((T, D), F32),
        compiler_params=_params("parallel"),
        name="proj_res",
    )(*ys, *ws, h, g)


def _norm_mm_kernel(x_ref, g_ref, w_ref, o_ref):
    o_ref[...] = _mm(_rms(x_ref[...], g_ref[...]), w_ref[...])


def _norm_mm(x, g, w, tm):
    T, D = x.shape
    N = w.shape[1]
    return pl.pallas_call(
        _norm_mm_kernel,
        grid=(T // tm,),
        in_specs=[pl.BlockSpec((tm, D), lambda i: (i, 0)), _const_spec((1, D)), _const_spec(w.shape)],
        out_specs=pl.BlockSpec((tm, N), lambda i: (i, 0)),
        out_shape=jax.ShapeDtypeStruct((T, N), F32),
        compiler_params=_params("parallel"),
        name="norm_mm",
    )(x, g, w)


def _xattn_kernel(h_ref, gpre_ref, wq_ref, kv_ref, wo_ref, gpost_ref, o_ref):
    h = h_ref[...]
    D = h.shape[1]
    hd = D // X_HEADS
    q = _mm(_rms(h, gpre_ref[...]), wq_ref[...])
    kv = kv_ref[...]
    outs = []
    for hh in range(X_HEADS):
        k = kv[:, hh * hd:(hh + 1) * hd]
        v = kv[:, D + hh * hd:D + (hh + 1) * hd]
        s = _mm_nt(q[:, hh * hd:(hh + 1) * hd], k) * (hd ** -0.5)
        p = jnp.exp(s - jnp.max(s, axis=-1, keepdims=True))
        p = p / jnp.sum(p, axis=-1, keepdims=True)
        outs.append(_mm(p, v))
    u = _mm(jnp.concatenate(outs, axis=1), wo_ref[...])
    o_ref[...] = h + _rms(u, gpost_ref[...])


def _xattn(h, gpre, wq, kv, wo, gpost, B, S):
    T, D = h.shape
    M = kv.shape[0] // B
    tm = ROW_TILE
    nt = S // tm
    return pl.pallas_call(
        _xattn_kernel,
        grid=(B, nt),
        in_specs=[pl.BlockSpec((tm, D), lambda b, j: (b * nt + j, 0)),
                  _const_spec((1, D)), _const_spec(wq.shape),
                  pl.BlockSpec((M, 2 * D), lambda b, j: (b, 0)),
                  _const_spec(wo.shape), _const_spec((1, D))],
        out_specs=pl.BlockSpec((tm, D), lambda b, j: (b * nt + j, 0)),
        out_shape=jax.ShapeDtypeStruct((T, D), F32),
        compiler_params=_params("parallel", "parallel"),
        name="xattn",
    )(h, gpre, wq, kv, wo, gpost)


def _mlp_kernel(h_ref, gpre_ref, w1_ref, w2_ref, gpost_ref, o_ref):
    h = h_ref[...]
    u = _rms(h, gpre_ref[...]).astype(MXU_DTYPE)
    dff = w1_ref.shape[1]
    acc = jnp.zeros(h.shape, F32)
    for c in range(dff // FF_CHUNK):
        sl = slice(c * FF_CHUNK, (c + 1) * FF_CHUNK)
        a = jnp.maximum(jnp.dot(u, w1_ref[:, sl], preferred_element_type=F32), 0.0)
        acc = acc + _mm(a * a, w2_ref[sl, :])
    o_ref[...] = h + _rms(acc, gpost_ref[...])


def _mlp(h, gpre, w1, w2, gpost):
    T, D = h.shape
    tm = ROW_TILE
    return pl.pallas_call(
        _mlp_kernel,
        grid=(T // tm,),
        in_specs=[pl.BlockSpec((tm, D), lambda i: (i, 0)), _const_spec((1, D)),
                  _const_spec(w1.shape), _const_spec(w2.shape), _const_spec((1, D))],
        out_specs=pl.BlockSpec((tm, D), lambda i: (i, 0)),
        out_shape=jax.ShapeDtypeStruct((T, D), F32),
        compiler_params=_params("parallel"),
        name="mlp",
    )(h, gpre, w1, w2, gpost)


def _rope(t, cos, sin_lo, sin_hi):
    n = t.shape[1]
    half = SW_HD // 2
    return t * cos + pltpu.roll(t, n - half, 1) * sin_lo + pltpu.roll(t, half, 1) * sin_hi


def _qkv_rope_kernel(h_ref, g_ref, w_ref, pos_ref, inv_ref, q_ref, k_ref, v_ref):
    tm = h_ref.shape[0]
    nq = q_ref.shape[1]
    nk = k_ref.shape[1]
    qkv = _mm(_rms(h_ref[...], g_ref[...]), w_ref[...])
    ang = pos_ref[...].astype(F32) * inv_ref[...]
    cos = jnp.cos(ang)
    sin = jnp.sin(ang)
    first_half = (lax.broadcasted_iota(jnp.int32, (tm, LANES), 1) & (SW_HD - 1)) < SW_HD // 2
    sin_lo = jnp.where(first_half, -sin, 0.0)
    sin_hi = jnp.where(first_half, 0.0, sin)

    def tiled(x, n):
        return jnp.concatenate([x] * (n // LANES), axis=1)

    q_ref[...] = _rope(qkv[:, :nq], tiled(cos, nq), tiled(sin_lo, nq), tiled(sin_hi, nq))
    k_ref[...] = _rope(qkv[:, nq:nq + nk], tiled(cos, nk), tiled(sin_lo, nk), tiled(sin_hi, nk))
    v_ref[...] = qkv[:, nq + nk:]


def _qkv_rope(h, g, w, pos, inv):
    T, D = h.shape
    tm = ROW_TILE
    nq = SW_HEADS * SW_HD
    nk = SW_KV_HEADS * SW_HD
    return pl.pallas_call(
        _qkv_rope_kernel,
        grid=(T // tm,),
        in_specs=[pl.BlockSpec((tm, D), lambda i: (i, 0)), _const_spec((1, D)), _const_spec(w.shape),
                  pl.BlockSpec((tm, 1), lambda i: (i, 0)), _const_spec((1, LANES))],
        out_specs=[pl.BlockSpec((tm, nq), lambda i: (i, 0)),
                   pl.BlockSpec((tm, nk), lambda i: (i, 0)),
                   pl.BlockSpec((tm, nk), lambda i: (i, 0))],
        out_shape=[jax.ShapeDtypeStruct((T, nq), F32), jax.ShapeDtypeStruct((T, nk), F32),
                   jax.ShapeDtypeStruct((T, nk), F32)],
        compiler_params=_params("parallel"),
        name="qkv_rope",
    )(h, g, w, pos, inv)


def _swa_kernel(sink_ref, q_ref, kp_ref, kc_ref, vp_ref, vc_ref, o_ref):
    j = pl.program_id(1)
    Lb = SW_BLOCK
    q = q_ref[...]
    k2 = jnp.concatenate([kp_ref[...], kc_ref[...]], axis=0)
    v2 = jnp.concatenate([vp_ref[...], vc_ref[...]], axis=0)
    rows = SW_GROUP * Lb
    r = lax.broadcasted_iota(jnp.int32, (rows, 2 * Lb), 0) & (Lb - 1)
    c = lax.broadcasted_iota(jnp.int32, (rows, 2 * Lb), 1)
    diff = r + Lb - c
    valid = (diff >= 0) & (diff < Lb) & ((c >= Lb) | (j > 0))
    outs = []
    for kk in range(SW_KV_HEADS):
        k = k2[:, kk * SW_HD:(kk + 1) * SW_HD]
        v = v2[:, kk * SW_HD:(kk + 1) * SW_HD]
        heads = range(kk * SW_GROUP, (kk + 1) * SW_GROUP)
        qs = jnp.concatenate([q[:, h * SW_HD:(h + 1) * SW_HD] for h in heads], axis=0)
        sink = jnp.concatenate([jnp.full((Lb, 1), sink_ref[h], F32) for h in heads], axis=0)
        s = jnp.where(valid, _mm_nt(qs, k) * (SW_HD ** -0.5), NEG_BIG)
        m = jnp.maximum(jnp.max(s, axis=-1, keepdims=True), sink)
        p = jnp.exp(s - m)
        den = jnp.sum(p, axis=-1, keepdims=True) + jnp.exp(sink - m)
        ob = _mm(p, v) / den
        outs.extend(ob[g * Lb:(g + 1) * Lb] for g in range(SW_GROUP))
    o_ref[...] = jnp.concatenate(outs, axis=1)


def _swa(q, k, v, sinks, B, S):
    T = q.shape[0]
    Lb = SW_BLOCK
    nb = S // Lb
    nq = SW_HEADS * SW_HD
    nk = SW_KV_HEADS * SW_HD

    def cur(b, j):
        return (b * nb + j, 0)

    def prev(b, j):
        return (b * nb + jnp.maximum(j - 1, 0), 0)

    return pl.pallas_call(
        _swa_kernel,
        grid=(B, nb),
        in_specs=[pl.BlockSpec(memory_space=pltpu.SMEM),
                  pl.BlockSpec((Lb, nq), cur),
                  pl.BlockSpec((Lb, nk), prev), pl.BlockSpec((Lb, nk), cur),
                  pl.BlockSpec((Lb, nk), prev), pl.BlockSpec((Lb, nk), cur)],
        out_specs=pl.BlockSpec((Lb, nq), cur),
        out_shape=jax.ShapeDtypeStruct((T, nq), F32),
        compiler_params=_params("parallel", "parallel"),
        name="swa",
    )(sinks, q, k, k, v, v)


def kernel(x, mem, positions, ev_w_in, ev_b_if, ev_conv_w, ev_conv_b, ev_rg_wa, ev_rg_ba, ev_rg_wx, ev_rg_bx,
           ev_rg_lambda, ev_w_out, od_w_qkv, od_sinks, od_w_o, g_mix_pre, g_mix_post, g_x_pre, g_x_post, g_mem,
           w_xq, w_xkv, w_xo, g_ff_pre, g_ff_post, w_ff1, w_ff2):
    B, S, D = x.shape
    T = B * S
    M = mem.shape[1]
    depth = g_mix_pre.shape[0]
    bf = MXU_DTYPE
    mlw = ML_HEADS * ML_HD
    rgw = RG_BLOCKS * RG_BW
    ng = 2 * ML_HEADS

    h = x.reshape(T, D)
    mem2 = mem.reshape(B * M, D)
    pos = positions.reshape(T, 1)
    half = SW_HD // 2
    inv = jnp.power(ROPE_THETA, -jnp.arange(half, dtype=F32) * 2.0 / SW_HD)
    inv = jnp.tile(inv, LANES // half)[None, :]

    def row(v):
        return v.reshape(1, -1)

    for l in range(depth):
        if l % 2 == 0:
            e = l // 2
            w_in = ev_w_in[e]
            wa = w_in[:, :4 * mlw].astype(bf)
            wg32 = w_in[:, 4 * mlw:4 * mlw + ng]
            wb = w_in[:, 4 * mlw + ng:].astype(bf)
            wg = jnp.pad(wg32, ((0, 0), (0, LANES - ng))).astype(bf)
            wgt = jnp.pad(wg32.T, ((0, GATE_ROWS - ng), (0, 0))).astype(bf)
            bc = row(ev_b_if[e])
            br = jnp.pad(ev_b_if[e], (0, GATE_ROWS - ng)).reshape(GATE_ROWS, 1)
            za, zb, gc, gr = _inproj(h, row(g_mix_pre[l]), wa, wb, wg, wgt, bc, br)
            ya = _mlstm(za, gc, gr, B, S)
            yb = _rglru(zb, ev_conv_w[e], row(ev_conv_b[e]), ev_rg_wa[e].astype(bf), row(ev_rg_ba[e]),
                        ev_rg_wx[e].astype(bf), row(ev_rg_bx[e]), row(ev_rg_lambda[e]), B, S)
            w_out = ev_w_out[e].astype(bf)
            h = _proj_res([ya, yb], [w_out[:mlw], w_out[mlw:]], h, row(g_mix_post[l]))
        else:
            o = l // 2
            q, k, v = _qkv_rope(h, row(g_mix_pre[l]), od_w_qkv[o].astype(bf), pos, inv)
            att = _swa(q, k, v, od_sinks[o], B, S)
            h = _proj_res([att], [od_w_o[o].astype(bf)], h, row(g_mix_post[l]))
        kv = _norm_mm(mem2, row(g_mem[l]), w_xkv[l].astype(bf), M)
        h = _xattn(h, row(g_x_pre[l]), w_xq[l].astype(bf), kv, w_xo[l].astype(bf), row(g_x_post[l]), B, S)
        h = _mlp(h, row(g_ff_pre[l]), w_ff1[l].astype(bf), w_ff2[l].astype(bf), row(g_ff_post[l]))
    return h.reshape(B, S, D)
```

```python
import functools
import math

import jax
import jax.numpy as jnp
from jax import lax
from jax.experimental import pallas as pl
from jax.experimental.pallas import tpu as pltpu

EPS = 1e-6
MXU_DTYPE = jnp.bfloat16
F32 = jnp.float32
LANES = 128
SUBLANES = 8
VMEM_LIMIT_BYTES = 56 * 1024 * 1024
NEG_BIG = -1e30
LOG2E = math.log2(math.e)

ML_HEADS = 4
ML_HD = 128
ML_CHUNK = 128
RG_BLOCKS = 4
RG_BW = 128
RG_CONV = 4
RG_C = 8.0
SW_HEADS = 16
SW_KV_HEADS = 4
SW_HD = 64
SW_GROUP = SW_HEADS // SW_KV_HEADS
SW_BLOCK = 128
SW_TQ = 512
ROPE_THETA = 10000.0
X_HEADS = 4
FF_CHUNK = 1024
ROW_TILE = 512
GATE_ROWS = 16


def _mm(a, b):
    return jnp.dot(a.astype(MXU_DTYPE), b.astype(MXU_DTYPE), preferred_element_type=F32)


def _mm_nt(a, b):
    return lax.dot_general(a.astype(MXU_DTYPE), b.astype(MXU_DTYPE),
                           (((1,), (1,)), ((), ())), preferred_element_type=F32)


def _rms(x, g):
    ms = jnp.mean(x * x, axis=-1, keepdims=True)
    return x * lax.rsqrt(ms + EPS) * g


def _softplus(x):
    return jnp.maximum(x, 0.0) + jnp.log1p(jnp.exp(-jnp.abs(x)))


def _log_sigmoid(x):
    return -_softplus(-x)


def _split3(x):
    p1 = x.astype(jnp.bfloat16)
    r1 = x - p1.astype(F32)
    p2 = r1.astype(jnp.bfloat16)
    r2 = r1 - p2.astype(F32)
    return p1, p2, r2.astype(jnp.bfloat16)


def _const_spec(shape):
    nd = len(shape)
    return pl.BlockSpec(shape, lambda *_: (0,) * nd, pipeline_mode=pl.Buffered(1))


def _params(*sem):
    return pltpu.CompilerParams(dimension_semantics=sem, vmem_limit_bytes=VMEM_LIMIT_BYTES)


def _inproj_kernel(h_ref, g_ref, wa_ref, wb_ref, wgt_ref, br_ref, za_ref, zb_ref, gr_ref):
    u = _rms(h_ref[...], g_ref[...]).astype(MXU_DTYPE)
    za_ref[...] = jnp.dot(u, wa_ref[...], preferred_element_type=F32)
    zb_ref[...] = jnp.dot(u, wb_ref[...], preferred_element_type=F32)
    gr_ref[...] = _mm_nt(wgt_ref[...], u) + br_ref[...]


def _inproj(h, g, wa, wb, wgt, br, B, S):
    T, D = h.shape
    tm = ROW_TILE
    nt = S // tm
    na, nb = wa.shape[1], wb.shape[1]
    return pl.pallas_call(
        _inproj_kernel,
        grid=(T // tm,),
        in_specs=[pl.BlockSpec((tm, D), lambda i: (i, 0)),
                  _const_spec((1, D)), _const_spec(wa.shape), _const_spec(wb.shape),
                  _const_spec(wgt.shape), _const_spec(br.shape)],
        out_specs=[pl.BlockSpec((tm, na), lambda i: (i, 0)),
                   pl.BlockSpec((tm, nb), lambda i: (i, 0)),
                   pl.BlockSpec((None, GATE_ROWS, tm), lambda i: (i // nt, 0, i % nt))],
        out_shape=[jax.ShapeDtypeStruct((T, na), F32), jax.ShapeDtypeStruct((T, nb), F32),
                   jax.ShapeDtypeStruct((B, GATE_ROWS, S), F32)],
        compiler_params=_params("parallel"),
        name="inproj",
    )(h, g, wa, wb, wgt, br)


def _split3_f32(x):
    t1 = x.astype(jnp.bfloat16).astype(F32)
    r1 = x - t1
    t2 = r1.astype(jnp.bfloat16).astype(F32)
    return t1, t2, r1 - t2


def _mlstm_kernel(q_ref, k_ref, v_ref, o_ref, gr_ref, y_ref, c_ref, m_ref):
    B, L, _ = q_ref.shape
    dh = ML_HD
    H = ML_HEADS
    scale = dh ** -0.5

    @pl.when(pl.program_id(0) == 0)
    def _():
        c_ref[...] = jnp.zeros_like(c_ref)
        m_ref[...] = jnp.zeros_like(m_ref)

    row = lax.broadcasted_iota(jnp.int32, (L, L), 0)
    col = lax.broadcasted_iota(jnp.int32, (L, L), 1)
    causal = col <= row
    tril = causal.astype(MXU_DTYPE)
    triu = (row <= col).astype(MXU_DTYPE)
    ones_half = jnp.ones((L, dh), MXU_DTYPE)

    gates = []
    for b in range(B):
        gr = gr_ref[b]
        parts = _split3_f32(_log_sigmoid(gr))
        b_r = sum(jnp.dot(p.astype(MXU_DTYPE), triu, preferred_element_type=F32) for p in parts)[H:2 * H]
        b_cb = sum(_mm_nt(tril, jnp.concatenate(
            [jnp.broadcast_to(p[H + h:H + h + 1], (LANES, L)) for h in range(H)], axis=0)) for p in parts)
        li_r = gr[0:H]
        m_prev = m_ref[b]
        b_last = b_r[:, L - 1:L]
        g_r = b_last - b_r + li_r
        m_new = jnp.maximum(b_last + m_prev, jnp.max(g_r, axis=1, keepdims=True))
        m_ref[b] = m_new
        gates.append(dict(b_cb=b_cb, e_r=li_r - b_r, m_prev=m_prev,
                          w_prev=jnp.exp(b_last + m_prev - m_new),
                          w_r=jnp.exp(g_r - m_new) * scale))

    chains = [(b, h) for b in range(B) for h in range(H)]

    def head(h):
        return slice(h * dh, (h + 1) * dh)

    def v_aug(b, h):
        return jnp.concatenate([v_ref[b, :, head(h)].astype(MXU_DTYPE), ones_half], axis=1)

    sqk, qc = {}, {}
    for b, h in chains:
        q = q_ref[b, :, head(h)].astype(MXU_DTYPE)
        sqk[b, h] = _mm_nt(q, k_ref[b, :, head(h)])
        qc[b, h] = _mm(q, c_ref[b * H + h])

    def weights(b, h):
        g = gates[b]
        b_t = g["b_cb"][:, h * LANES:(h + 1) * LANES]
        mp = jnp.broadcast_to(g["m_prev"][h:h + 1], (L, LANES))
        dmat = jnp.where(causal, b_t + g["e_r"][h:h + 1], NEG_BIG)
        m_t = jnp.maximum(b_t + mp, jnp.max(dmat, axis=1, keepdims=True))
        s = sqk[b, h] * (jnp.exp(dmat - m_t) * scale)
        return s, m_t, jnp.exp(b_t + mp - m_t)

    def finish(b, h, sv, m_t, decay):
        tot = sv + jnp.concatenate([decay, decay], axis=1) * qc[b, h]
        den = jnp.maximum(jnp.abs(tot[:, dh:]), jnp.exp(-m_t))
        y = jax.nn.sigmoid(o_ref[b, :, head(h)]) * (tot[:, :dh] / den)
        y_ref[b, :, head(h)] = y.astype(y_ref.dtype)

    pending = None
    for b, h in chains:
        s, m_t, decay = weights(b, h)
        if pending is not None:
            finish(*pending)
        pending = (b, h, _mm(s, v_aug(b, h)), m_t, decay)
    finish(*pending)

    for b, h in chains:
        g = gates[b]
        kw_t = k_ref[b, :, head(h)].T * g["w_r"][h:h + 1]
        c_ref[b * H + h] = g["w_prev"][h:h + 1] * c_ref[b * H + h] + _mm(kw_t, v_aug(b, h))


def _mlstm(za, gr, B, S):
    L = ML_CHUNK
    assert L == LANES
    W = ML_HEADS * ML_HD
    za3 = za.reshape(B, S, za.shape[1])

    def col(j):
        return pl.BlockSpec((B, L, W), lambda c: (0, c, j))

    y = pl.pallas_call(
        _mlstm_kernel,
        grid=(S // L,),
        in_specs=[col(0), col(1), col(2), col(3),
                  pl.BlockSpec((B, GATE_ROWS, L), lambda c: (0, 0, c))],
        out_specs=pl.BlockSpec((B, L, W), lambda c: (0, c, 0)),
        out_shape=jax.ShapeDtypeStruct((B, S, W), MXU_DTYPE),
        scratch_shapes=[pltpu.VMEM((B * ML_HEADS, ML_HD, 2 * ML_HD), F32),
                        pltpu.VMEM((B, ML_HEADS, 1), F32)],
        compiler_params=_params("arbitrary"),
        name="mlstm",
    )(za3, za3, za3, za3, gr)
    return y.reshape(B * S, W)


def _sigmoid_tanh(x):
    return 0.5 + 0.5 * jnp.tanh(0.5 * x)


def _one_minus_sq(a, y):
    series = -y * (1.0 + y * (1.0 / 2 + y * (1.0 / 6 + y * (1.0 / 24 + y * (1.0 / 120)))))
    return jnp.where(y > -2.0 ** -5, series, 1.0 - a * a)


def _gelu_tanh(x):
    return 0.5 * x * (1.0 + jnp.tanh(math.sqrt(2.0 / math.pi) * (x + 0.044715 * (x * x * x))))


def _rglru_kernel(rx_ref, ry_ref, cw_ref, cb_ref, wa_ref, ba_ref, wx_ref, bx_ref, lam_ref,
                  y_ref, tail_ref, hc_ref, a_s, u_s):
    ts, W = rx_ref.shape
    ng = ts // SUBLANES

    @pl.when(pl.program_id(1) == 0)
    def _():
        tail_ref[...] = jnp.zeros_like(tail_ref)
        hc_ref[...] = jnp.zeros_like(hc_ref)

    rowmod = lax.broadcasted_iota(jnp.int32, (ng, SUBLANES, W), 1)
    x = rx_ref[...]
    x3 = x.reshape(ng, SUBLANES, W)
    prev = tail_ref[...]
    cw = cw_ref[...]
    xc3 = x3 * cw[RG_CONV - 1:RG_CONV] + cb_ref[...]
    for d in range(1, RG_CONV):
        xr = pltpu.roll(x3, d, 1)
        xr_prev = jnp.concatenate([pltpu.roll(prev, d, 0)[None], xr[:-1]], axis=0)
        xc3 = xc3 + jnp.where(rowmod >= d, xr, xr_prev) * cw[RG_CONV - 1 - d:RG_CONV - d]
    tail_ref[...] = x[ts - SUBLANES:ts]
    xc = xc3.reshape(ts, W)

    xcb = xc.astype(MXU_DTYPE)
    ga = jnp.concatenate([jnp.dot(xcb[:, n * RG_BW:(n + 1) * RG_BW], wa_ref[n], preferred_element_type=F32)
                          for n in range(RG_BLOCKS)], axis=1) + ba_ref[...]
    gx = jnp.concatenate([jnp.dot(xcb[:, n * RG_BW:(n + 1) * RG_BW], wx_ref[n], preferred_element_type=F32)
                          for n in range(RG_BLOCKS)], axis=1) + bx_ref[...]
    log_a = (-RG_C * _softplus(-lam_ref[...])) * _sigmoid_tanh(ga)
    a = jnp.exp(log_a)
    u = jnp.sqrt(_one_minus_sq(a, 2.0 * log_a)) * (_sigmoid_tanh(gx) * xc)

    a3 = a.reshape(ng, SUBLANES, W)
    u3 = u.reshape(ng, SUBLANES, W)
    for d in (1, 2, 4):
        keep = rowmod >= d
        a_sh = jnp.where(keep, pltpu.roll(a3, d, 1), 1.0)
        u_sh = jnp.where(keep, pltpu.roll(u3, d, 1), 0.0)
        u3 = a3 * u_sh + u3
        a3 = a3 * a_sh
    a_s[...] = a3.reshape(ts, W)
    u_s[...] = u3.reshape(ts, W)

    def group(gi, h_in):
        sl = pl.ds(pl.multiple_of(gi * SUBLANES, SUBLANES), SUBLANES)
        ag = a_s[sl, :]
        ug = u_s[sl, :]
        u_s[sl, :] = ag * h_in + ug
        a_tot = jnp.broadcast_to(ag[SUBLANES - 1:SUBLANES], (SUBLANES, W))
        u_tot = jnp.broadcast_to(ug[SUBLANES - 1:SUBLANES], (SUBLANES, W))
        return a_tot * h_in + u_tot

    hc_ref[...] = lax.fori_loop(0, ng, group, hc_ref[...], unroll=8)
    y_ref[...] = (_gelu_tanh(ry_ref[...]) * u_s[...]).astype(y_ref.dtype)


def _rglru(zb, cw, cb, wa, ba, wx, bx, lam, B, S):
    T = zb.shape[0]
    W = RG_BLOCKS * RG_BW
    ts = ROW_TILE
    nt = S // ts
    return pl.pallas_call(
        _rglru_kernel,
        grid=(B, nt),
        in_specs=[pl.BlockSpec((ts, W), lambda b, j: (b * nt + j, 0)),
                  pl.BlockSpec((ts, W), lambda b, j: (b * nt + j, 1)),
                  _const_spec(cw.shape), _const_spec(cb.shape), _const_spec(wa.shape), _const_spec(ba.shape),
                  _const_spec(wx.shape), _const_spec(bx.shape), _const_spec(lam.shape)],
        out_specs=pl.BlockSpec((ts, W), lambda b, j: (b * nt + j, 0)),
        out_shape=jax.ShapeDtypeStruct((T, W), MXU_DTYPE),
        scratch_shapes=[pltpu.VMEM((SUBLANES, W), F32), pltpu.VMEM((SUBLANES, W), F32),
                        pltpu.VMEM((ts, W), F32), pltpu.VMEM((ts, W), F32)],
        compiler_params=_params("parallel", "arbitrary"),
        name="rglru",
    )(zb, zb, cw, cb, wa, ba, wx, bx, lam)


def _proj_res_kernel(*refs, n):
    ys, ws = refs[:n], refs[n:2 * n]
    h_ref, g_ref, o_ref = refs[2 * n:]
    acc = _mm(ys[0][...], ws[0][...])
    for y, w in zip(ys[1:], ws[1:]):
        acc = acc + _mm(y[...], w[...])
    o_ref[...] = h_ref[...] + _rms(acc, g_ref[...])


def _proj_res(ys, ws, h, g):
    T, D = h.shape
    tm = ROW_TILE
    n = len(ys)
    return pl.pallas_call(
        functools.partial(_proj_res_kernel, n=n),
        grid=(T // tm,),
        in_specs=([pl.BlockSpec((tm, y.shape[1]), lambda i: (i, 0)) for y in ys]
                  + [_const_spec(w.shape) for w in ws]
                  + [pl.BlockSpec((tm, D), lambda i: (i, 0)), _const_spec((1, D))]),
        out_specs=pl.BlockSpec((tm, D), lambda i: (i, 0)),
        out_shape=jax.ShapeDtypeStruct((T, D), F32),
        compiler_params=_params("parallel"),
        name="proj_res",
    )(*ys, *ws, h, g)


def _norm_mm_kernel(x_ref, g_ref, w_ref, o_ref):
    o_ref[...] = _mm(_rms(x_ref[...], g_ref[...]), w_ref[...])


def _norm_mm(x, g, w, tm):
    T, D = x.shape
    N = w.shape[1]
    return pl.pallas_call(
        _norm_mm_kernel,
        grid=(T // tm,),
        in_specs=[pl.BlockSpec((tm, D), lambda i: (i, 0)), _const_spec((1, D)), _const_spec(w.shape)],
        out_specs=pl.BlockSpec((tm, N), lambda i: (i, 0)),
        out_shape=jax.ShapeDtypeStruct((T, N), F32),
        compiler_params=_params("parallel"),
        name="norm_mm",
    )(x, g, w)


def _xattn_kernel(h_ref, gpre_ref, wq_ref, kv_ref, wo_ref, gpost_ref, o_ref):
    h = h_ref[...]
    D = h.shape[1]
    hd = D // X_HEADS
    q = _mm(_rms(h, gpre_ref[...]), wq_ref[...])
    kv = kv_ref[...]
    outs = []
    for hh in range(X_HEADS):
        k = kv[:, hh * hd:(hh + 1) * hd]
        v = kv[:, D + hh * hd:D + (hh + 1) * hd]
        s = _mm_nt(q[:, hh * hd:(hh + 1) * hd], k) * (hd ** -0.5)
        p = jnp.exp(s - jnp.max(s, axis=-1, keepdims=True))
        p = p / jnp.sum(p, axis=-1, keepdims=True)
        outs.append(_mm(p, v))
    u = _mm(jnp.concatenate(outs, axis=1), wo_ref[...])
    o_ref[...] = h + _rms(u, gpost_ref[...])


def _xattn(h, gpre, wq, kv, wo, gpost, B, S):
    T, D = h.shape
    M = kv.shape[0] // B
    tm = ROW_TILE
    nt = S // tm
    return pl.pallas_call(
        _xattn_kernel,
        grid=(B, nt),
        in_specs=[pl.BlockSpec((tm, D), lambda b, j: (b * nt + j, 0)),
                  _const_spec((1, D)), _const_spec(wq.shape),
                  pl.BlockSpec((M, 2 * D), lambda b, j: (b, 0)),
                  _const_spec(wo.shape), _const_spec((1, D))],
        out_specs=pl.BlockSpec((tm, D), lambda b, j: (b * nt + j, 0)),
        out_shape=jax.ShapeDtypeStruct((T, D), F32),
        compiler_params=_params("parallel", "parallel"),
        name="xattn",
    )(h, gpre, wq, kv, wo, gpost)


def _mlp_kernel(h_ref, gpre_ref, w1_ref, w2_ref, gpost_ref, o_ref):
    h = h_ref[...]
    u = _rms(h, gpre_ref[...]).astype(MXU_DTYPE)
    dff = w1_ref.shape[1]
    acc = jnp.zeros(h.shape, F32)
    for c in range(dff // FF_CHUNK):
        sl = slice(c * FF_CHUNK, (c + 1) * FF_CHUNK)
        a = jnp.maximum(jnp.dot(u, w1_ref[:, sl], preferred_element_type=F32), 0.0)
        acc = acc + _mm(a * a, w2_ref[sl, :])
    o_ref[...] = h + _rms(acc, gpost_ref[...])


def _mlp(h, gpre, w1, w2, gpost):
    T, D = h.shape
    tm = ROW_TILE
    return pl.pallas_call(
        _mlp_kernel,
        grid=(T // tm,),
        in_specs=[pl.BlockSpec((tm, D), lambda i: (i, 0)), _const_spec((1, D)),
                  _const_spec(w1.shape), _const_spec(w2.shape), _const_spec((1, D))],
        out_specs=pl.BlockSpec((tm, D), lambda i: (i, 0)),
        out_shape=jax.ShapeDtypeStruct((T, D), F32),
        compiler_params=_params("parallel"),
        name="mlp",
    )(h, gpre, w1, w2, gpost)


def _rope(t, cos, sin_lo, sin_hi):
    n = t.shape[1]
    half = SW_HD // 2
    return t * cos + pltpu.roll(t, n - half, 1) * sin_lo + pltpu.roll(t, half, 1) * sin_hi


def _head_pair_split(t):
    tm, n = t.shape
    lo_mask = lax.broadcasted_iota(jnp.int32, (tm, LANES), 1) < SW_HD
    los, his = [], []
    for s in range(n // LANES):
        slab = t[:, s * LANES:(s + 1) * LANES]
        swapped = pltpu.roll(slab, SW_HD, 1)
        los += [jnp.where(lo_mask, slab, 0.0), jnp.where(lo_mask, swapped, 0.0)]
        his += [jnp.where(lo_mask, 0.0, swapped), jnp.where(lo_mask, 0.0, slab)]
    return jnp.concatenate(los, axis=1), jnp.concatenate(his, axis=1)


def _qkv_rope_kernel(h_ref, g_ref, w_ref, pos_ref, inv_ref, q_ref, klo_ref, khi_ref, vlo_ref, vhi_ref):
    tm = h_ref.shape[0]
    nq = q_ref.shape[1]
    nk = SW_KV_HEADS * SW_HD
    qkv = _mm(_rms(h_ref[...], g_ref[...]), w_ref[...])
    ang = pos_ref[...].astype(F32) * inv_ref[...]
    cos = jnp.cos(ang)
    sin = jnp.sin(ang)
    first_half = (lax.broadcasted_iota(jnp.int32, (tm, LANES), 1) & (SW_HD - 1)) < SW_HD // 2
    sin_lo = jnp.where(first_half, -sin, 0.0)
    sin_hi = jnp.where(first_half, 0.0, sin)

    def tiled(x, n):
        return jnp.concatenate([x] * (n // LANES), axis=1)

    q = _rope(qkv[:, :nq], tiled(cos, nq), tiled(sin_lo, nq), tiled(sin_hi, nq))
    q_ref[...] = (q * (SW_HD ** -0.5 * LOG2E)).astype(q_ref.dtype)
    k = _rope(qkv[:, nq:nq + nk], tiled(cos, nk), tiled(sin_lo, nk), tiled(sin_hi, nk))
    klo, khi = _head_pair_split(k)
    klo_ref[...] = klo.astype(klo_ref.dtype)
    khi_ref[...] = khi.astype(khi_ref.dtype)
    vlo, vhi = _head_pair_split(qkv[:, nq + nk:])
    vlo_ref[...] = vlo.astype(vlo_ref.dtype)
    vhi_ref[...] = vhi.astype(vhi_ref.dtype)


def _qkv_rope(h, g, w, pos, inv):
    T, D = h.shape
    tm = ROW_TILE
    nq = SW_HEADS * SW_HD
    nkp = SW_KV_HEADS * LANES
    kv_spec = pl.BlockSpec((tm, nkp), lambda i: (i, 0))
    kv_shape = jax.ShapeDtypeStruct((T, nkp), MXU_DTYPE)
    return pl.pallas_call(
        _qkv_rope_kernel,
        grid=(T // tm,),
        in_specs=[pl.BlockSpec((tm, D), lambda i: (i, 0)), _const_spec((1, D)), _const_spec(w.shape),
                  pl.BlockSpec((tm, 1), lambda i: (i, 0)), _const_spec((1, LANES))],
        out_specs=[pl.BlockSpec((tm, nq), lambda i: (i, 0)), kv_spec, kv_spec, kv_spec, kv_spec],
        out_shape=[jax.ShapeDtypeStruct((T, nq), MXU_DTYPE), kv_shape, kv_shape, kv_shape, kv_shape],
        compiler_params=_params("parallel"),
        name="qkv_rope",
    )(h, g, w, pos, inv)


def _swa_kernel(sink_ref, q_ref, klp_ref, klc_ref, khp_ref, khc_ref, vlp_ref, vlc_ref, vhp_ref, vhc_ref, o_ref):
    Lb = SW_BLOCK
    nblk = q_ref.shape[0] // Lb
    kr = lax.broadcasted_iota(jnp.int32, (4 * Lb, LANES), 0)
    kc = lax.broadcasted_iota(jnp.int32, (4 * Lb, LANES), 1)
    key = kr & (2 * Lb - 1)
    diff = kc + Lb - key
    band = (diff >= 0) & (diff < Lb)
    first_lim = jnp.where(pl.program_id(1) == 0, Lb, 0)
    bias_t = jnp.where(band, 0.0, NEG_BIG)
    bias_t_first = jnp.where(key >= first_lim, bias_t, NEG_BIG)
    bias_t = bias_t.astype(MXU_DTYPE)
    bias_t_first = bias_t_first.astype(MXU_DTYPE)
    qr = lax.broadcasted_iota(jnp.int32, (2 * Lb, LANES), 0)
    qc = lax.broadcasted_iota(jnp.int32, (2 * Lb, LANES), 1)
    row_onehot = ((qr & (Lb - 1)) == qc).astype(MXU_DTYPE)
    ones_ext = ((kc >= SW_HD) == (kr >= 2 * Lb)).astype(MXU_DTYPE)
    sink_col = lax.broadcasted_iota(jnp.int32, (Lb, LANES), 1) == 0
    sink_row = lax.broadcasted_iota(jnp.int32, (2 * SUBLANES, LANES), 0) == 0

    def window(p_ref, c_ref, blk, cols):
        if blk == 0:
            return jnp.concatenate([p_ref[:, cols], c_ref[0:Lb, cols]], axis=0)
        return c_ref[(blk - 1) * Lb:(blk + 1) * Lb, cols]

    def value_window(p_ref, c_ref, blk, cols):
        w = window(p_ref, c_ref, blk, cols)
        top = jnp.where(sink_row, jnp.zeros_like(w[0:2 * SUBLANES]), w[0:2 * SUBLANES])
        return jnp.concatenate([top, w[2 * SUBLANES:]], axis=0)

    def scores(blk, kk):
        cols = slice(kk * LANES, (kk + 1) * LANES)
        rows = slice(blk * Lb, (blk + 1) * Lb)
        kbd = jnp.concatenate([window(klp_ref, klc_ref, blk, cols), window(khp_ref, khc_ref, blk, cols)], axis=0)
        kbd = jnp.concatenate([kbd, bias_t_first if blk == 0 else bias_t], axis=1)
        q2 = jnp.concatenate([q_ref[rows, (2 * kk + a) * LANES:(2 * kk + a + 1) * LANES] for a in range(2)], axis=0)
        q2 = jnp.concatenate([q2, row_onehot], axis=1)
        return _mm_nt(q2, kbd)

    def finish(blk, kk, s):
        cols = slice(kk * LANES, (kk + 1) * LANES)
        rows = slice(blk * Lb, (blk + 1) * Lb)
        vbd = jnp.concatenate([value_window(vlp_ref, vlc_ref, blk, cols), value_window(vhp_ref, vhc_ref, blk, cols)],
                              axis=0)
        vbd = jnp.concatenate([vbd, ones_ext], axis=1)
        p_rows = []
        for a in range(2):
            p_cols = []
            for hb in range(2):
                sink = sink_ref[SW_GROUP * kk + 2 * a + hb] * LOG2E
                c0 = hb * 2 * Lb
                sq = jnp.concatenate([jnp.where(sink_col, sink, s[a * Lb:(a + 1) * Lb, c0:c0 + LANES]),
                                      s[a * Lb:(a + 1) * Lb, c0 + LANES:c0 + 2 * Lb]], axis=1)
                m = jnp.max(sq, axis=-1, keepdims=True)
                p_cols.append(jnp.exp2(sq - m).astype(MXU_DTYPE))
            p_rows.append(jnp.concatenate(p_cols, axis=1))
        ov = _mm(jnp.concatenate(p_rows, axis=0), vbd)
        for a in range(2):
            oa = ov[a * Lb:(a + 1) * Lb]
            o_ref[rows, (2 * kk + a) * LANES:(2 * kk + a + 1) * LANES] = (
                oa[:, :LANES] / oa[:, LANES:]).astype(o_ref.dtype)

    units = [(blk, kk) for blk in range(nblk) for kk in range(SW_KV_HEADS)]
    s_next = scores(*units[0])
    for i, unit in enumerate(units):
        s_cur = s_next
        if i + 1 < len(units):
            s_next = scores(*units[i + 1])
        finish(*unit, s_cur)


def _swa(q, klo, khi, vlo, vhi, sinks, B, S):
    T, nq = q.shape
    Lb = SW_BLOCK
    tq = SW_TQ
    nt = S // tq
    per = tq // Lb
    nkp = SW_KV_HEADS * LANES

    def cur(b, j):
        return (b * nt + j, 0)

    def prev(b, j):
        return (b * nt * per + jnp.maximum(j * per - 1, 0), 0)

    p_spec = pl.BlockSpec((Lb, nkp), prev)
    c_spec = pl.BlockSpec((tq, nkp), cur)
    return pl.pallas_call(
        _swa_kernel,
        grid=(B, nt),
        in_specs=[pl.BlockSpec(memory_space=pltpu.SMEM), pl.BlockSpec((tq, nq), cur),
                  p_spec, c_spec, p_spec, c_spec, p_spec, c_spec, p_spec, c_spec],
        out_specs=pl.BlockSpec((tq, nq), cur),
        out_shape=jax.ShapeDtypeStruct((T, nq), MXU_DTYPE),
        compiler_params=_params("parallel", "parallel"),
        name="swa",
    )(sinks, q, klo, klo, khi, khi, vlo, vlo, vhi, vhi)


def kernel(x, mem, positions, ev_w_in, ev_b_if, ev_conv_w, ev_conv_b, ev_rg_wa, ev_rg_ba, ev_rg_wx, ev_rg_bx,
           ev_rg_lambda, ev_w_out, od_w_qkv, od_sinks, od_w_o, g_mix_pre, g_mix_post, g_x_pre, g_x_post, g_mem,
           w_xq, w_xkv, w_xo, g_ff_pre, g_ff_post, w_ff1, w_ff2):
    B, S, D = x.shape
    T = B * S
    M = mem.shape[1]
    depth = g_mix_pre.shape[0]
    bf = MXU_DTYPE
    mlw = ML_HEADS * ML_HD
    rgw = RG_BLOCKS * RG_BW
    ng = 2 * ML_HEADS

    h = x.reshape(T, D)
    mem2 = mem.reshape(B * M, D)
    pos = positions.reshape(T, 1)
    half = SW_HD // 2
    inv = jnp.power(ROPE_THETA, -jnp.arange(half, dtype=F32) * 2.0 / SW_HD)
    inv = jnp.tile(inv, LANES // half)[None, :]

    def row(v):
        return v.reshape(1, -1)

    for l in range(depth):
        if l % 2 == 0:
            e = l // 2
            w_in = ev_w_in[e]
            wa = w_in[:, :4 * mlw].astype(bf)
            wg32 = w_in[:, 4 * mlw:4 * mlw + ng]
            wb = w_in[:, 4 * mlw + ng:].astype(bf)
            wgt = jnp.pad(wg32.T, ((0, GATE_ROWS - ng), (0, 0))).astype(bf)
            br = jnp.pad(ev_b_if[e], (0, GATE_ROWS - ng)).reshape(GATE_ROWS, 1)
            za, zb, gr = _inproj(h, row(g_mix_pre[l]), wa, wb, wgt, br, B, S)
            ya = _mlstm(za, gr, B, S)
            yb = _rglru(zb, ev_conv_w[e], row(ev_conv_b[e]), ev_rg_wa[e].astype(bf), row(ev_rg_ba[e]),
                        ev_rg_wx[e].astype(bf), row(ev_rg_bx[e]), row(ev_rg_lambda[e]), B, S)
            w_out = ev_w_out[e].astype(bf)
            h = _proj_res([ya, yb], [w_out[:mlw], w_out[mlw:]], h, row(g_mix_post[l]))
        else:
            o = l // 2
            q, klo, khi, vlo, vhi = _qkv_rope(h, row(g_mix_pre[l]), od_w_qkv[o].astype(bf), pos, inv)
            att = _swa(q, klo, khi, vlo, vhi, od_sinks[o], B, S)
            h = _proj_res([att], [od_w_o[o].astype(bf)], h, row(g_mix_post[l]))
        kv = _norm_mm(mem2, row(g_mem[l]), w_xkv[l].astype(bf), M)
        h = _xattn(h, row(g_x_pre[l]), w_xq[l].astype(bf), kv, w_xo[l].astype(bf), row(g_x_post[l]), B, S)
        h = _mlp(h, row(g_ff_pre[l]), w_ff1[l].astype(bf), w_ff2[l].astype(bf), row(g_ff_post[l]))
    return h.reshape(B, S, D)
```

```python
import functools
import math

import jax
import jax.numpy as jnp
from jax import lax
from jax.experimental import pallas as pl
from jax.experimental.pallas import tpu as pltpu

EPS = 1e-6
MXU_DTYPE = jnp.bfloat16
F32 = jnp.float32
LANES = 128
SUBLANES = 8
VMEM_LIMIT_BYTES = 56 * 1024 * 1024
NEG_BIG = -1e30
LOG2E = math.log2(math.e)

ML_HEADS = 4
ML_HD = 128
ML_CHUNK = 128
RG_BLOCKS = 4
RG_BW = 128
RG_CONV = 4
RG_C = 8.0
SW_HEADS = 16
SW_KV_HEADS = 4
SW_HD = 64
SW_GROUP = SW_HEADS // SW_KV_HEADS
SW_BLOCK = 128
SW_TQ = 512
ROPE_THETA = 10000.0
X_HEADS = 4
FF_CHUNK = 1024
ROW_TILE = 512
GATE_ROWS = 16


def _mm(a, b):
    return jnp.dot(a.astype(MXU_DTYPE), b.astype(MXU_DTYPE), preferred_element_type=F32)


def _mm_nt(a, b):
    return lax.dot_general(a.astype(MXU_DTYPE), b.astype(MXU_DTYPE),
                           (((1,), (1,)), ((), ())), preferred_element_type=F32)


def _rms(x, g):
    ms = jnp.mean(x * x, axis=-1, keepdims=True)
    return x * lax.rsqrt(ms + EPS) * g


def _softplus(x):
    return jnp.maximum(x, 0.0) + jnp.log1p(jnp.exp(-jnp.abs(x)))


def _log_sigmoid(x):
    return -_softplus(-x)


def _split3(x):
    p1 = x.astype(jnp.bfloat16)
    r1 = x - p1.astype(F32)
    p2 = r1.astype(jnp.bfloat16)
    r2 = r1 - p2.astype(F32)
    return p1, p2, r2.astype(jnp.bfloat16)


def _const_spec(shape):
    nd = len(shape)
    return pl.BlockSpec(shape, lambda *_: (0,) * nd, pipeline_mode=pl.Buffered(1))


def _params(*sem):
    return pltpu.CompilerParams(dimension_semantics=sem, vmem_limit_bytes=VMEM_LIMIT_BYTES)


def _inproj_kernel(h_ref, g_ref, wa_ref, wb_ref, wgt_ref, br_ref, za_ref, zb_ref, gr_ref):
    u = _rms(h_ref[...], g_ref[...]).astype(MXU_DTYPE)
    za_ref[...] = jnp.dot(u, wa_ref[...], preferred_element_type=F32)
    zb_ref[...] = jnp.dot(u, wb_ref[...], preferred_element_type=F32)
    gr_ref[...] = _mm_nt(wgt_ref[...], u) + br_ref[...]


def _inproj(h, g, wa, wb, wgt, br, B, S):
    T, D = h.shape
    tm = ROW_TILE
    nt = S // tm
    na, nb = wa.shape[1], wb.shape[1]
    return pl.pallas_call(
        _inproj_kernel,
        grid=(T // tm,),
        in_specs=[pl.BlockSpec((tm, D), lambda i: (i, 0)),
                  _const_spec((1, D)), _const_spec(wa.shape), _const_spec(wb.shape),
                  _const_spec(wgt.shape), _const_spec(br.shape)],
        out_specs=[pl.BlockSpec((tm, na), lambda i: (i, 0)),
                   pl.BlockSpec((tm, nb), lambda i: (i, 0)),
                   pl.BlockSpec((None, GATE_ROWS, tm), lambda i: (i // nt, 0, i % nt))],
        out_shape=[jax.ShapeDtypeStruct((T, na), F32), jax.ShapeDtypeStruct((T, nb), F32),
                   jax.ShapeDtypeStruct((B, GATE_ROWS, S), F32)],
        compiler_params=_params("parallel"),
        name="inproj",
    )(h, g, wa, wb, wgt, br)


def _split3_f32(x):
    t1 = x.astype(jnp.bfloat16).astype(F32)
    r1 = x - t1
    t2 = r1.astype(jnp.bfloat16).astype(F32)
    return t1, t2, r1 - t2


def _mlstm_kernel(q_ref, k_ref, v_ref, o_ref, gr_ref, y_ref, c_ref, m_ref):
    B, L, _ = q_ref.shape
    dh = ML_HD
    H = ML_HEADS
    scale = dh ** -0.5

    @pl.when(pl.program_id(0) == 0)
    def _():
        c_ref[...] = jnp.zeros_like(c_ref)
        m_ref[...] = jnp.zeros_like(m_ref)

    row = lax.broadcasted_iota(jnp.int32, (L, L), 0)
    col = lax.broadcasted_iota(jnp.int32, (L, L), 1)
    causal = col <= row
    tril = causal.astype(MXU_DTYPE)
    triu = (row <= col).astype(MXU_DTYPE)
    ones_half = jnp.ones((L, dh), MXU_DTYPE)

    gates = []
    for b in range(B):
        gr = gr_ref[b]
        parts = _split3_f32(_log_sigmoid(gr))
        b_r = sum(jnp.dot(p.astype(MXU_DTYPE), triu, preferred_element_type=F32) for p in parts)[H:2 * H]
        b_cb = sum(_mm_nt(tril, jnp.concatenate(
            [jnp.broadcast_to(p[H + h:H + h + 1], (LANES, L)) for h in range(H)], axis=0)) for p in parts)
        li_r = gr[0:H]
        m_prev = m_ref[b]
        b_last = b_r[:, L - 1:L]
        g_r = b_last - b_r + li_r
        m_new = jnp.maximum(b_last + m_prev, jnp.max(g_r, axis=1, keepdims=True))
        m_ref[b] = m_new
        gates.append(dict(b_cb=b_cb, e_r=li_r - b_r, m_prev=m_prev,
                          w_prev=jnp.exp(b_last + m_prev - m_new),
                          w_r=jnp.exp(g_r - m_new) * scale))

    chains = [(b, h) for b in range(B) for h in range(H)]

    def head(h):
        return slice(h * dh, (h + 1) * dh)

    def v_aug(b, h):
        return jnp.concatenate([v_ref[b, :, head(h)].astype(MXU_DTYPE), ones_half], axis=1)

    sqk, qc = {}, {}
    for b, h in chains:
        q = q_ref[b, :, head(h)].astype(MXU_DTYPE)
        sqk[b, h] = _mm_nt(q, k_ref[b, :, head(h)])
        qc[b, h] = _mm(q, c_ref[b * H + h])

    def weights(b, h):
        g = gates[b]
        b_t = g["b_cb"][:, h * LANES:(h + 1) * LANES]
        mp = jnp.broadcast_to(g["m_prev"][h:h + 1], (L, LANES))
        dmat = jnp.where(causal, b_t + g["e_r"][h:h + 1], NEG_BIG)
        m_t = jnp.maximum(b_t + mp, jnp.max(dmat, axis=1, keepdims=True))
        s = sqk[b, h] * (jnp.exp(dmat - m_t) * scale)
        return s, m_t, jnp.exp(b_t + mp - m_t)

    def finish(b, h, sv, m_t, decay):
        tot = sv + jnp.concatenate([decay, decay], axis=1) * qc[b, h]
        den = jnp.maximum(jnp.abs(tot[:, dh:]), jnp.exp(-m_t))
        y = jax.nn.sigmoid(o_ref[b, :, head(h)]) * (tot[:, :dh] / den)
        y_ref[b, :, head(h)] = y.astype(y_ref.dtype)

    pending = None
    for b, h in chains:
        s, m_t, decay = weights(b, h)
        if pending is not None:
            finish(*pending)
        pending = (b, h, _mm(s, v_aug(b, h)), m_t, decay)
    finish(*pending)

    for b, h in chains:
        g = gates[b]
        kw_t = k_ref[b, :, head(h)].T * g["w_r"][h:h + 1]
        c_ref[b * H + h] = g["w_prev"][h:h + 1] * c_ref[b * H + h] + _mm(kw_t, v_aug(b, h))


def _mlstm(za, gr, B, S):
    L = ML_CHUNK
    assert L == LANES
    W = ML_HEADS * ML_HD
    za3 = za.reshape(B, S, za.shape[1])

    def col(j):
        return pl.BlockSpec((B, L, W), lambda c: (0, c, j))

    y = pl.pallas_call(
        _mlstm_kernel,
        grid=(S // L,),
        in_specs=[col(0), col(1), col(2), col(3),
                  pl.BlockSpec((B, GATE_ROWS, L), lambda c: (0, 0, c))],
        out_specs=pl.BlockSpec((B, L, W), lambda c: (0, c, 0)),
        out_shape=jax.ShapeDtypeStruct((B, S, W), MXU_DTYPE),
        scratch_shapes=[pltpu.VMEM((B * ML_HEADS, ML_HD, 2 * ML_HD), F32),
                        pltpu.VMEM((B, ML_HEADS, 1), F32)],
        compiler_params=_params("arbitrary"),
        name="mlstm",
    )(za3, za3, za3, za3, gr)
    return y.reshape(B * S, W)


def _sigmoid_tanh(x):
    return 0.5 + 0.5 * jnp.tanh(0.5 * x)


def _one_minus_sq(a, y):
    series = -y * (1.0 + y * (1.0 / 2 + y * (1.0 / 6 + y * (1.0 / 24 + y * (1.0 / 120)))))
    return jnp.where(y > -2.0 ** -5, series, 1.0 - a * a)


def _gelu_tanh(x):
    return 0.5 * x * (1.0 + jnp.tanh(math.sqrt(2.0 / math.pi) * (x + 0.044715 * (x * x * x))))


def _rglru_kernel(rx_ref, ry_ref, cw_ref, cb_ref, wa_ref, ba_ref, wx_ref, bx_ref, lam_ref,
                  y_ref, tail_ref, hc_ref, a_s, u_s):
    ts, W = rx_ref.shape
    ng = ts // SUBLANES

    @pl.when(pl.program_id(1) == 0)
    def _():
        tail_ref[...] = jnp.zeros_like(tail_ref)
        hc_ref[...] = jnp.zeros_like(hc_ref)

    rowmod = lax.broadcasted_iota(jnp.int32, (ng, SUBLANES, W), 1)
    x = rx_ref[...]
    x3 = x.reshape(ng, SUBLANES, W)
    prev = tail_ref[...]
    cw = cw_ref[...]
    xc3 = x3 * cw[RG_CONV - 1:RG_CONV] + cb_ref[...]
    for d in range(1, RG_CONV):
        xr = pltpu.roll(x3, d, 1)
        xr_prev = jnp.concatenate([pltpu.roll(prev, d, 0)[None], xr[:-1]], axis=0)
        xc3 = xc3 + jnp.where(rowmod >= d, xr, xr_prev) * cw[RG_CONV - 1 - d:RG_CONV - d]
    tail_ref[...] = x[ts - SUBLANES:ts]
    xc = xc3.reshape(ts, W)

    xcb = xc.astype(MXU_DTYPE)
    ga = jnp.concatenate([jnp.dot(xcb[:, n * RG_BW:(n + 1) * RG_BW], wa_ref[n], preferred_element_type=F32)
                          for n in range(RG_BLOCKS)], axis=1) + ba_ref[...]
    gx = jnp.concatenate([jnp.dot(xcb[:, n * RG_BW:(n + 1) * RG_BW], wx_ref[n], preferred_element_type=F32)
                          for n in range(RG_BLOCKS)], axis=1) + bx_ref[...]
    log_a = (-RG_C * _softplus(-lam_ref[...])) * _sigmoid_tanh(ga)
    a = jnp.exp(log_a)
    u = jnp.sqrt(_one_minus_sq(a, 2.0 * log_a)) * (_sigmoid_tanh(gx) * xc)

    a3 = a.reshape(ng, SUBLANES, W)
    u3 = u.reshape(ng, SUBLANES, W)
    for d in (1, 2, 4):
        keep = rowmod >= d
        a_sh = jnp.where(keep, pltpu.roll(a3, d, 1), 1.0)
        u_sh = jnp.where(keep, pltpu.roll(u3, d, 1), 0.0)
        u3 = a3 * u_sh + u3
        a3 = a3 * a_sh
    a_s[...] = a3.reshape(ts, W)
    u_s[...] = u3.reshape(ts, W)

    def group(gi, h_in):
        sl = pl.ds(pl.multiple_of(gi * SUBLANES, SUBLANES), SUBLANES)
        ag = a_s[sl, :]
        ug = u_s[sl, :]
        u_s[sl, :] = ag * h_in + ug
        a_tot = jnp.broadcast_to(ag[SUBLANES - 1:SUBLANES], (SUBLANES, W))
        u_tot = jnp.broadcast_to(ug[SUBLANES - 1:SUBLANES], (SUBLANES, W))
        return a_tot * h_in + u_tot

    hc_ref[...] = lax.fori_loop(0, ng, group, hc_ref[...], unroll=8)
    y_ref[...] = (_gelu_tanh(ry_ref[...]) * u_s[...]).astype(y_ref.dtype)


def _rglru(zb, cw, cb, wa, ba, wx, bx, lam, B, S):
    T = zb.shape[0]
    W = RG_BLOCKS * RG_BW
    ts = ROW_TILE
    nt = S // ts
    return pl.pallas_call(
        _rglru_kernel,
        grid=(B, nt),
        in_specs=[pl.BlockSpec((ts, W), lambda b, j: (b * nt + j, 0)),
                  pl.BlockSpec((ts, W), lambda b, j: (b * nt + j, 1)),
                  _const_spec(cw.shape), _const_spec(cb.shape), _const_spec(wa.shape), _const_spec(ba.shape),
                  _const_spec(wx.shape), _const_spec(bx.shape), _const_spec(lam.shape)],
        out_specs=pl.BlockSpec((ts, W), lambda b, j: (b * nt + j, 0)),
        out_shape=jax.ShapeDtypeStruct((T, W), MXU_DTYPE),
        scratch_shapes=[pltpu.VMEM((SUBLANES, W), F32), pltpu.VMEM((SUBLANES, W), F32),
                        pltpu.VMEM((ts, W), F32), pltpu.VMEM((ts, W), F32)],
        compiler_params=_params("parallel", "arbitrary"),
        name="rglru",
    )(zb, zb, cw, cb, wa, ba, wx, bx, lam)


def _tail_kernel(*refs, n):
    ys, ws = refs[:n], refs[n:2 * n]
    (h_ref, gmix_ref, gxpre_ref, wq_ref, kv_ref, wo_ref, gxpost_ref,
     gfpre_ref, w1_ref, w2_ref, gfpost_ref, o_ref) = refs[2 * n:]
    D = h_ref.shape[1]

    acc = _mm(ys[0][...], ws[0][...])
    for y, w in zip(ys[1:], ws[1:]):
        acc = acc + _mm(y[...], w[...])
    h = h_ref[...] + _rms(acc, gmix_ref[...])

    hd = D // X_HEADS
    q = (_mm(_rms(h, gxpre_ref[...]), wq_ref[...]) * (hd ** -0.5 * LOG2E)).astype(MXU_DTYPE)

    def scores(hh):
        return _mm_nt(q[:, hh * hd:(hh + 1) * hd], kv_ref[:, hh * hd:(hh + 1) * hd])

    outs = []
    s_next = scores(0)
    for hh in range(X_HEADS):
        s = s_next
        if hh + 1 < X_HEADS:
            s_next = scores(hh + 1)
        p = jnp.exp2(s - jnp.max(s, axis=-1, keepdims=True))
        p = p / jnp.sum(p, axis=-1, keepdims=True)
        outs.append(_mm(p, kv_ref[:, D + hh * hd:D + (hh + 1) * hd]).astype(MXU_DTYPE))
    u = _mm(jnp.concatenate(outs, axis=1), wo_ref[...])
    h = h + _rms(u, gxpost_ref[...])

    u = _rms(h, gfpre_ref[...]).astype(MXU_DTYPE)
    dff = w1_ref.shape[1]
    acc = jnp.zeros(h.shape, F32)
    for c in range(dff // FF_CHUNK):
        sl = slice(c * FF_CHUNK, (c + 1) * FF_CHUNK)
        a = jnp.maximum(jnp.dot(u, w1_ref[:, sl], preferred_element_type=F32), 0.0)
        acc = acc + _mm(a * a, w2_ref[sl, :])
    o_ref[...] = h + _rms(acc, gfpost_ref[...])


def _tail(ys, ws, h, gmix, gxpre, wq, kv, wo, gxpost, gfpre, w1, w2, gfpost, B, S):
    T, D = h.shape
    M = kv.shape[0] // B
    tm = ROW_TILE
    nt = S // tm
    n = len(ys)

    def rows(width):
        return pl.BlockSpec((tm, width), lambda b, j: (b * nt + j, 0))

    g_spec = _const_spec((1, D))
    return pl.pallas_call(
        functools.partial(_tail_kernel, n=n),
        grid=(B, nt),
        in_specs=([rows(y.shape[1]) for y in ys] + [_const_spec(w.shape) for w in ws]
                  + [rows(D), g_spec, g_spec, _const_spec(wq.shape),
                     pl.BlockSpec((M, 2 * D), lambda b, j: (b, 0)),
                     _const_spec(wo.shape), g_spec, g_spec, _const_spec(w1.shape), _const_spec(w2.shape), g_spec]),
        out_specs=rows(D),
        out_shape=jax.ShapeDtypeStruct((T, D), F32),
        compiler_params=_params("parallel", "parallel"),
        name="tail",
    )(*ys, *ws, h, gmix, gxpre, wq, kv, wo, gxpost, gfpre, w1, w2, gfpost)


def _norm_mm_kernel(x_ref, g_ref, w_ref, o_ref):
    o_ref[...] = _mm(_rms(x_ref[...], g_ref[...]), w_ref[...]).astype(o_ref.dtype)


def _norm_mm(x, g, w, tm):
    T, D = x.shape
    N = w.shape[1]
    return pl.pallas_call(
        _norm_mm_kernel,
        grid=(T // tm,),
        in_specs=[pl.BlockSpec((tm, D), lambda i: (i, 0)), _const_spec((1, D)), _const_spec(w.shape)],
        out_specs=pl.BlockSpec((tm, N), lambda i: (i, 0)),
        out_shape=jax.ShapeDtypeStruct((T, N), MXU_DTYPE),
        compiler_params=_params("parallel"),
        name="norm_mm",
    )(x, g, w)


def _rope(t, cos, sin_lo, sin_hi):
    n = t.shape[1]
    half = SW_HD // 2
    return t * cos + pltpu.roll(t, n - half, 1) * sin_lo + pltpu.roll(t, half, 1) * sin_hi


def _head_pair_split(t):
    tm, n = t.shape
    lo_mask = lax.broadcasted_iota(jnp.int32, (tm, LANES), 1) < SW_HD
    los, his = [], []
    for s in range(n // LANES):
        slab = t[:, s * LANES:(s + 1) * LANES]
        swapped = pltpu.roll(slab, SW_HD, 1)
        los += [jnp.where(lo_mask, slab, 0.0), jnp.where(lo_mask, swapped, 0.0)]
        his += [jnp.where(lo_mask, 0.0, swapped), jnp.where(lo_mask, 0.0, slab)]
    return jnp.concatenate(los, axis=1), jnp.concatenate(his, axis=1)


def _qkv_rope_kernel(h_ref, g_ref, w_ref, pos_ref, inv_ref, q_ref, klo_ref, khi_ref, vlo_ref, vhi_ref):
    tm = h_ref.shape[0]
    nq = q_ref.shape[1]
    nk = SW_KV_HEADS * SW_HD
    qkv = _mm(_rms(h_ref[...], g_ref[...]), w_ref[...])
    ang = pos_ref[...].astype(F32) * inv_ref[...]
    cos = jnp.cos(ang)
    sin = jnp.sin(ang)
    first_half = (lax.broadcasted_iota(jnp.int32, (tm, LANES), 1) & (SW_HD - 1)) < SW_HD // 2
    sin_lo = jnp.where(first_half, -sin, 0.0)
    sin_hi = jnp.where(first_half, 0.0, sin)

    def tiled(x, n):
        return jnp.concatenate([x] * (n // LANES), axis=1)

    q = _rope(qkv[:, :nq], tiled(cos, nq), tiled(sin_lo, nq), tiled(sin_hi, nq))
    q_ref[...] = (q * (SW_HD ** -0.5 * LOG2E)).astype(q_ref.dtype)
    k = _rope(qkv[:, nq:nq + nk], tiled(cos, nk), tiled(sin_lo, nk), tiled(sin_hi, nk))
    klo, khi = _head_pair_split(k)
    klo_ref[...] = klo.astype(klo_ref.dtype)
    khi_ref[...] = khi.astype(khi_ref.dtype)
    vlo, vhi = _head_pair_split(qkv[:, nq + nk:])
    vlo_ref[...] = vlo.astype(vlo_ref.dtype)
    vhi_ref[...] = vhi.astype(vhi_ref.dtype)


def _qkv_rope(h, g, w, pos, inv):
    T, D = h.shape
    tm = ROW_TILE
    nq = SW_HEADS * SW_HD
    nkp = SW_KV_HEADS * LANES
    kv_spec = pl.BlockSpec((tm, nkp), lambda i: (i, 0))
    kv_shape = jax.ShapeDtypeStruct((T, nkp), MXU_DTYPE)
    return pl.pallas_call(
        _qkv_rope_kernel,
        grid=(T // tm,),
        in_specs=[pl.BlockSpec((tm, D), lambda i: (i, 0)), _const_spec((1, D)), _const_spec(w.shape),
                  pl.BlockSpec((tm, 1), lambda i: (i, 0)), _const_spec((1, LANES))],
        out_specs=[pl.BlockSpec((tm, nq), lambda i: (i, 0)), kv_spec, kv_spec, kv_spec, kv_spec],
        out_shape=[jax.ShapeDtypeStruct((T, nq), MXU_DTYPE), kv_shape, kv_shape, kv_shape, kv_shape],
        compiler_params=_params("parallel"),
        name="qkv_rope",
    )(h, g, w, pos, inv)


def _swa_kernel(sink_ref, q_ref, klp_ref, klc_ref, khp_ref, khc_ref, vlp_ref, vlc_ref, vhp_ref, vhc_ref, o_ref):
    Lb = SW_BLOCK
    nblk = q_ref.shape[0] // Lb
    kr = lax.broadcasted_iota(jnp.int32, (4 * Lb, LANES), 0)
    kc = lax.broadcasted_iota(jnp.int32, (4 * Lb, LANES), 1)
    key = kr & (2 * Lb - 1)
    diff = kc + Lb - key
    band = (diff >= 0) & (diff < Lb)
    first_lim = jnp.where(pl.program_id(1) == 0, Lb, 0)
    bias_t = jnp.where(band, 0.0, NEG_BIG)
    bias_t_first = jnp.where(key >= first_lim, bias_t, NEG_BIG)
    bias_t = bias_t.astype(MXU_DTYPE)
    bias_t_first = bias_t_first.astype(MXU_DTYPE)
    qr = lax.broadcasted_iota(jnp.int32, (2 * Lb, LANES), 0)
    qc = lax.broadcasted_iota(jnp.int32, (2 * Lb, LANES), 1)
    row_onehot = ((qr & (Lb - 1)) == qc).astype(MXU_DTYPE)
    ones_ext = ((kc >= SW_HD) == (kr >= 2 * Lb)).astype(MXU_DTYPE)
    sink_col = lax.broadcasted_iota(jnp.int32, (Lb, LANES), 1) == 0
    sink_row = lax.broadcasted_iota(jnp.int32, (2 * SUBLANES, LANES), 0) == 0

    def window(p_ref, c_ref, blk, cols):
        if blk == 0:
            return jnp.concatenate([p_ref[:, cols], c_ref[0:Lb, cols]], axis=0)
        return c_ref[(blk - 1) * Lb:(blk + 1) * Lb, cols]

    def value_window(p_ref, c_ref, blk, cols):
        w = window(p_ref, c_ref, blk, cols)
        top = jnp.where(sink_row, jnp.zeros_like(w[0:2 * SUBLANES]), w[0:2 * SUBLANES])
        return jnp.concatenate([top, w[2 * SUBLANES:]], axis=0)

    def scores(blk, kk):
        cols = slice(kk * LANES, (kk + 1) * LANES)
        rows = slice(blk * Lb, (blk + 1) * Lb)
        kbd = jnp.concatenate([window(klp_ref, klc_ref, blk, cols), window(khp_ref, khc_ref, blk, cols)], axis=0)
        kbd = jnp.concatenate([kbd, bias_t_first if blk == 0 else bias_t], axis=1)
        q2 = jnp.concatenate([q_ref[rows, (2 * kk + a) * LANES:(2 * kk + a + 1) * LANES] for a in range(2)], axis=0)
        q2 = jnp.concatenate([q2, row_onehot], axis=1)
        return _mm_nt(q2, kbd)

    def finish(blk, kk, s):
        cols = slice(kk * LANES, (kk + 1) * LANES)
        rows = slice(blk * Lb, (blk + 1) * Lb)
        vbd = jnp.concatenate([value_window(vlp_ref, vlc_ref, blk, cols), value_window(vhp_ref, vhc_ref, blk, cols)],
                              axis=0)
        vbd = jnp.concatenate([vbd, ones_ext], axis=1)
        p_rows = []
        for a in range(2):
            p_cols = []
            for hb in range(2):
                sink = sink_ref[SW_GROUP * kk + 2 * a + hb] * LOG2E
                c0 = hb * 2 * Lb
                sq = jnp.concatenate([jnp.where(sink_col, sink, s[a * Lb:(a + 1) * Lb, c0:c0 + LANES]),
                                      s[a * Lb:(a + 1) * Lb, c0 + LANES:c0 + 2 * Lb]], axis=1)
                m = jnp.max(sq, axis=-1, keepdims=True)
                p_cols.append(jnp.exp2(sq - m).astype(MXU_DTYPE))
            p_rows.append(jnp.concatenate(p_cols, axis=1))
        ov = _mm(jnp.concatenate(p_rows, axis=0), vbd)
        for a in range(2):
            oa = ov[a * Lb:(a + 1) * Lb]
            o_ref[rows, (2 * kk + a) * LANES:(2 * kk + a + 1) * LANES] = (
                oa[:, :LANES] / oa[:, LANES:]).astype(o_ref.dtype)

    units = [(blk, kk) for blk in range(nblk) for kk in range(SW_KV_HEADS)]
    s_next = scores(*units[0])
    for i, unit in enumerate(units):
        s_cur = s_next
        if i + 1 < len(units):
            s_next = scores(*units[i + 1])
        finish(*unit, s_cur)


def _swa(q, klo, khi, vlo, vhi, sinks, B, S):
    T, nq = q.shape
    Lb = SW_BLOCK
    tq = SW_TQ
    nt = S // tq
    per = tq // Lb
    nkp = SW_KV_HEADS * LANES

    def cur(b, j):
        return (b * nt + j, 0)

    def prev(b, j):
        return (b * nt * per + jnp.maximum(j * per - 1, 0), 0)

    p_spec = pl.BlockSpec((Lb, nkp), prev)
    c_spec = pl.BlockSpec((tq, nkp), cur)
    return pl.pallas_call(
        _swa_kernel,
        grid=(B, nt),
        in_specs=[pl.BlockSpec(memory_space=pltpu.SMEM), pl.BlockSpec((tq, nq), cur),
                  p_spec, c_spec, p_spec, c_spec, p_spec, c_spec, p_spec, c_spec],
        out_specs=pl.BlockSpec((tq, nq), cur),
        out_shape=jax.ShapeDtypeStruct((T, nq), MXU_DTYPE),
        compiler_params=_params("parallel", "parallel"),
        name="swa",
    )(sinks, q, klo, klo, khi, khi, vlo, vlo, vhi, vhi)


def kernel(x, mem, positions, ev_w_in, ev_b_if, ev_conv_w, ev_conv_b, ev_rg_wa, ev_rg_ba, ev_rg_wx, ev_rg_bx,
           ev_rg_lambda, ev_w_out, od_w_qkv, od_sinks, od_w_o, g_mix_pre, g_mix_post, g_x_pre, g_x_post, g_mem,
           w_xq, w_xkv, w_xo, g_ff_pre, g_ff_post, w_ff1, w_ff2):
    B, S, D = x.shape
    T = B * S
    M = mem.shape[1]
    depth = g_mix_pre.shape[0]
    bf = MXU_DTYPE
    mlw = ML_HEADS * ML_HD
    rgw = RG_BLOCKS * RG_BW
    ng = 2 * ML_HEADS

    h = x.reshape(T, D)
    mem2 = mem.reshape(B * M, D)
    pos = positions.reshape(T, 1)
    half = SW_HD // 2
    inv = jnp.power(ROPE_THETA, -jnp.arange(half, dtype=F32) * 2.0 / SW_HD)
    inv = jnp.tile(inv, LANES // half)[None, :]

    def row(v):
        return v.reshape(1, -1)

    for l in range(depth):
        if l % 2 == 0:
            e = l // 2
            w_in = ev_w_in[e]
            wa = w_in[:, :4 * mlw].astype(bf)
            wg32 = w_in[:, 4 * mlw:4 * mlw + ng]
            wb = w_in[:, 4 * mlw + ng:].astype(bf)
            wgt = jnp.pad(wg32.T, ((0, GATE_ROWS - ng), (0, 0))).astype(bf)
            br = jnp.pad(ev_b_if[e], (0, GATE_ROWS - ng)).reshape(GATE_ROWS, 1)
            za, zb, gr = _inproj(h, row(g_mix_pre[l]), wa, wb, wgt, br, B, S)
            ya = _mlstm(za, gr, B, S)
            yb = _rglru(zb, ev_conv_w[e], row(ev_conv_b[e]), ev_rg_wa[e].astype(bf), row(ev_rg_ba[e]),
                        ev_rg_wx[e].astype(bf), row(ev_rg_bx[e]), row(ev_rg_lambda[e]), B, S)
            w_out = ev_w_out[e].astype(bf)
            ys, ws = [ya, yb], [w_out[:mlw], w_out[mlw:]]
        else:
            o = l // 2
            q, klo, khi, vlo, vhi = _qkv_rope(h, row(g_mix_pre[l]), od_w_qkv[o].astype(bf), pos, inv)
            ys, ws = [_swa(q, klo, khi, vlo, vhi, od_sinks[o], B, S)], [od_w_o[o].astype(bf)]
        kv = _norm_mm(mem2, row(g_mem[l]), w_xkv[l].astype(bf), M)
        h = _tail(ys, ws, h, row(g_mix_post[l]), row(g_x_pre[l]), w_xq[l].astype(bf), kv, w_xo[l].astype(bf),
                  row(g_x_post[l]), row(g_ff_pre[l]), w_ff1[l].astype(bf), w_ff2[l].astype(bf), row(g_ff_post[l]),
                  B, S)
    return h.reshape(B, S, D)
```

```python
import functools
import math

import jax
import jax.numpy as jnp
from jax import lax
from jax.experimental import pallas as pl
from jax.experimental.pallas import tpu as pltpu

EPS = 1e-6
MXU_DTYPE = jnp.bfloat16
F32 = jnp.float32
LANES = 128
SUBLANES = 8
VMEM_LIMIT_BYTES = 56 * 1024 * 1024
NEG_BIG = -1e30
LOG2E = math.log2(math.e)

ML_HEADS = 4
ML_HD = 128
ML_CHUNK = 128
RG_BLOCKS = 4
RG_BW = 128
RG_CONV = 4
RG_C = 8.0
SW_HEADS = 16
SW_KV_HEADS = 4
SW_HD = 64
SW_GROUP = SW_HEADS // SW_KV_HEADS
SW_BLOCK = 128
SW_TQ = 512
ROPE_THETA = 10000.0
X_HEADS = 4
FF_CHUNK = 1024
ROW_TILE = 512
TAIL_SPLIT = 2
INPROJ_CHUNKS = 4
GATE_ROWS = 16


def _mm(a, b):
    return jnp.dot(a.astype(MXU_DTYPE), b.astype(MXU_DTYPE), preferred_element_type=F32)


def _mm_nt(a, b):
    return lax.dot_general(a.astype(MXU_DTYPE), b.astype(MXU_DTYPE),
                           (((1,), (1,)), ((), ())), preferred_element_type=F32)


def _rms(x, g):
    ms = jnp.mean(x * x, axis=-1, keepdims=True)
    return x * lax.rsqrt(ms + EPS) * g


def _softplus(x):
    return jnp.maximum(x, 0.0) + jnp.log1p(jnp.exp(-jnp.abs(x)))


def _log_sigmoid(x):
    return -_softplus(-x)


def _split3(x):
    p1 = x.astype(jnp.bfloat16)
    r1 = x - p1.astype(F32)
    p2 = r1.astype(jnp.bfloat16)
    r2 = r1 - p2.astype(F32)
    return p1, p2, r2.astype(jnp.bfloat16)


def _const_spec(shape):
    nd = len(shape)
    return pl.BlockSpec(shape, lambda *_: (0,) * nd, pipeline_mode=pl.Buffered(1))


def _params(*sem):
    return pltpu.CompilerParams(dimension_semantics=sem, vmem_limit_bytes=VMEM_LIMIT_BYTES)


def _sigmoid_tanh(x):
    return 0.5 + 0.5 * jnp.tanh(0.5 * x)


def _one_minus_sq(a, y):
    series = -y * (1.0 + y * (1.0 / 2 + y * (1.0 / 6 + y * (1.0 / 24 + y * (1.0 / 120)))))
    return jnp.where(y > -2.0 ** -5, series, 1.0 - a * a)


def _gelu_tanh(x):
    return 0.5 * x * (1.0 + jnp.tanh(math.sqrt(2.0 / math.pi) * (x + 0.044715 * (x * x * x))))


def _inproj_kernel(h_ref, g_ref, wa_ref, wb_ref, wgt_ref, br_ref,
                   cw_ref, cb_ref, rwa_ref, rba_ref, rwx_ref, rbx_ref, lam_ref,
                   za_ref, gr_ref, yb_ref, tail_ref, hc_ref, *, tiles_per_seq):
    tm = h_ref.shape[0]
    W = RG_BLOCKS * RG_BW
    ng = tm // SUBLANES

    @pl.when(pl.program_id(0) % tiles_per_seq == 0)
    def _():
        tail_ref[...] = jnp.zeros_like(tail_ref)
        hc_ref[...] = jnp.zeros_like(hc_ref)

    u_in = _rms(h_ref[...], g_ref[...]).astype(MXU_DTYPE)
    zb = jnp.dot(u_in, wb_ref[...], preferred_element_type=F32)
    gr_ref[...] = _mm_nt(wgt_ref[...], u_in) + br_ref[...]
    chunk = wa_ref.shape[1] // INPROJ_CHUNKS

    def za_chunk(c):
        sl = slice(c * chunk, (c + 1) * chunk)
        za_ref[:, sl] = jnp.dot(u_in, wa_ref[:, sl], preferred_element_type=F32)

    za_chunk(0)
    rowmod = lax.broadcasted_iota(jnp.int32, (ng, SUBLANES, W), 1)
    x = zb[:, :W]
    x3 = x.reshape(ng, SUBLANES, W)
    prev = tail_ref[...]
    cw = cw_ref[...]
    xc3 = x3 * cw[RG_CONV - 1:RG_CONV] + cb_ref[...]
    for d in range(1, RG_CONV):
        xr = pltpu.roll(x3, d, 1)
        xr_prev = jnp.concatenate([pltpu.roll(prev, d, 0)[None], xr[:-1]], axis=0)
        xc3 = xc3 + jnp.where(rowmod >= d, xr, xr_prev) * cw[RG_CONV - 1 - d:RG_CONV - d]
    tail_ref[...] = x[tm - SUBLANES:tm]
    xc = xc3.reshape(tm, W)

    xcb = xc.astype(MXU_DTYPE)
    ga = jnp.concatenate([jnp.dot(xcb[:, n * RG_BW:(n + 1) * RG_BW], rwa_ref[n], preferred_element_type=F32)
                          for n in range(RG_BLOCKS)], axis=1) + rba_ref[...]
    gx = jnp.concatenate([jnp.dot(xcb[:, n * RG_BW:(n + 1) * RG_BW], rwx_ref[n], preferred_element_type=F32)
                          for n in range(RG_BLOCKS)], axis=1) + rbx_ref[...]
    za_chunk(1)
    log_a = (-RG_C * _softplus(-lam_ref[...])) * _sigmoid_tanh(ga)
    a = jnp.exp(log_a)
    u = jnp.sqrt(_one_minus_sq(a, 2.0 * log_a)) * (_sigmoid_tanh(gx) * xc)

    za_chunk(2)
    a3 = a.reshape(ng, SUBLANES, W)
    u3 = u.reshape(ng, SUBLANES, W)
    for d in (1, 2, 4):
        keep = rowmod >= d
        a_sh = jnp.where(keep, pltpu.roll(a3, d, 1), 1.0)
        u_sh = jnp.where(keep, pltpu.roll(u3, d, 1), 0.0)
        u3 = a3 * u_sh + u3
        a3 = a3 * a_sh

    za_chunk(3)
    h_in = hc_ref[...]
    hs = []
    for gi in range(ng):
        hs.append(a3[gi] * h_in + u3[gi])
        a_tot = jnp.broadcast_to(a3[gi, SUBLANES - 1:SUBLANES], (SUBLANES, W))
        u_tot = jnp.broadcast_to(u3[gi, SUBLANES - 1:SUBLANES], (SUBLANES, W))
        h_in = a_tot * h_in + u_tot
    hc_ref[...] = h_in
    h_rg = jnp.concatenate(hs, axis=0)
    yb_ref[...] = (_gelu_tanh(zb[:, W:]) * h_rg).astype(yb_ref.dtype)


def _inproj(h, g, wa, wb, wgt, br, cw, cb, rwa, rba, rwx, rbx, lam, B, S):
    T, D = h.shape
    tm = ROW_TILE
    nt = S // tm
    na = wa.shape[1]
    W = RG_BLOCKS * RG_BW
    consts = [g, wa, wb, wgt, br, cw, cb, rwa, rba, rwx, rbx, lam]
    return pl.pallas_call(
        functools.partial(_inproj_kernel, tiles_per_seq=nt),
        grid=(T // tm,),
        in_specs=[pl.BlockSpec((tm, D), lambda i: (i, 0))] + [_const_spec(c.shape) for c in consts],
        out_specs=[pl.BlockSpec((tm, na), lambda i: (i, 0)),
                   pl.BlockSpec((None, GATE_ROWS, tm), lambda i: (i // nt, 0, i % nt)),
                   pl.BlockSpec((tm, W), lambda i: (i, 0))],
        out_shape=[jax.ShapeDtypeStruct((T, na), F32),
                   jax.ShapeDtypeStruct((B, GATE_ROWS, S), F32),
                   jax.ShapeDtypeStruct((T, W), MXU_DTYPE)],
        scratch_shapes=[pltpu.VMEM((SUBLANES, W), F32), pltpu.VMEM((SUBLANES, W), F32)],
        compiler_params=_params("arbitrary"),
        name="inproj",
    )(h, *consts)


def _split3_f32(x):
    t1 = x.astype(jnp.bfloat16).astype(F32)
    r1 = x - t1
    t2 = r1.astype(jnp.bfloat16).astype(F32)
    return t1, t2, r1 - t2


def _mlstm_kernel(q_ref, k_ref, v_ref, o_ref, gr_ref, y_ref, c_ref, m_ref):
    B, L, _ = q_ref.shape
    dh = ML_HD
    H = ML_HEADS
    scale = dh ** -0.5

    @pl.when(pl.program_id(0) == 0)
    def _():
        c_ref[...] = jnp.zeros_like(c_ref)
        m_ref[...] = jnp.zeros_like(m_ref)

    row = lax.broadcasted_iota(jnp.int32, (L, L), 0)
    col = lax.broadcasted_iota(jnp.int32, (L, L), 1)
    causal = col <= row
    tril = causal.astype(MXU_DTYPE)
    triu = (row <= col).astype(MXU_DTYPE)
    ones_half = jnp.ones((L, dh), MXU_DTYPE)

    gates = []
    for b in range(B):
        gr = gr_ref[b]
        parts = _split3_f32(_log_sigmoid(gr))
        b_r = sum(jnp.dot(p.astype(MXU_DTYPE), triu, preferred_element_type=F32) for p in parts)[H:2 * H]
        b_cb = sum(_mm_nt(tril, jnp.concatenate(
            [jnp.broadcast_to(p[H + h:H + h + 1], (LANES, L)) for h in range(H)], axis=0)) for p in parts)
        li_r = gr[0:H]
        m_prev = m_ref[b]
        b_last = b_r[:, L - 1:L]
        g_r = b_last - b_r + li_r
        m_new = jnp.maximum(b_last + m_prev, jnp.max(g_r, axis=1, keepdims=True))
        m_ref[b] = m_new
        gates.append(dict(b_cb=b_cb, e_r=li_r - b_r, m_prev=m_prev,
                          w_prev=jnp.exp(b_last + m_prev - m_new),
                          w_r=jnp.exp(g_r - m_new) * scale))

    chains = [(b, h) for b in range(B) for h in range(H)]

    def head(h):
        return slice(h * dh, (h + 1) * dh)

    def v_aug(b, h):
        return jnp.concatenate([v_ref[b, :, head(h)].astype(MXU_DTYPE), ones_half], axis=1)

    sqk, qc = {}, {}
    for b, h in chains:
        q = q_ref[b, :, head(h)].astype(MXU_DTYPE)
        sqk[b, h] = _mm_nt(q, k_ref[b, :, head(h)])
        qc[b, h] = _mm(q, c_ref[b * H + h])

    def weights(b, h):
        g = gates[b]
        b_t = g["b_cb"][:, h * LANES:(h + 1) * LANES]
        mp = jnp.broadcast_to(g["m_prev"][h:h + 1], (L, LANES))
        dmat = jnp.where(causal, b_t + g["e_r"][h:h + 1], NEG_BIG)
        m_t = jnp.maximum(b_t + mp, jnp.max(dmat, axis=1, keepdims=True))
        s = sqk[b, h] * (jnp.exp(dmat - m_t) * scale)
        return s, m_t, jnp.exp(b_t + mp - m_t)

    def finish(b, h, sv, m_t, decay):
        tot = sv + jnp.concatenate([decay, decay], axis=1) * qc[b, h]
        den = jnp.maximum(jnp.abs(tot[:, dh:]), jnp.exp(-m_t))
        y = jax.nn.sigmoid(o_ref[b, :, head(h)]) * (tot[:, :dh] / den)
        y_ref[b, :, head(h)] = y.astype(y_ref.dtype)

    pending = None
    for b, h in chains:
        s, m_t, decay = weights(b, h)
        if pending is not None:
            finish(*pending)
        pending = (b, h, _mm(s, v_aug(b, h)), m_t, decay)
    finish(*pending)

    for b, h in chains:
        g = gates[b]
        kw_t = k_ref[b, :, head(h)].T * g["w_r"][h:h + 1]
        c_ref[b * H + h] = g["w_prev"][h:h + 1] * c_ref[b * H + h] + _mm(kw_t, v_aug(b, h))


def _mlstm(za, gr, B, S):
    L = ML_CHUNK
    assert L == LANES
    W = ML_HEADS * ML_HD
    za3 = za.reshape(B, S, za.shape[1])

    def col(j):
        return pl.BlockSpec((B, L, W), lambda c: (0, c, j))

    y = pl.pallas_call(
        _mlstm_kernel,
        grid=(S // L,),
        in_specs=[col(0), col(1), col(2), col(3),
                  pl.BlockSpec((B, GATE_ROWS, L), lambda c: (0, 0, c))],
        out_specs=pl.BlockSpec((B, L, W), lambda c: (0, c, 0)),
        out_shape=jax.ShapeDtypeStruct((B, S, W), MXU_DTYPE),
        scratch_shapes=[pltpu.VMEM((B * ML_HEADS, ML_HD, 2 * ML_HD), F32),
                        pltpu.VMEM((B, ML_HEADS, 1), F32)],
        compiler_params=_params("arbitrary"),
        name="mlstm",
    )(za3, za3, za3, za3, gr)
    return y.reshape(B * S, W)


def _tail_kernel(*refs, n):
    ys, ws = refs[:n], refs[n:2 * n]
    (h_ref, gmix_ref, gxpre_ref, wq_ref, kv_ref, wo_ref, gxpost_ref,
     gfpre_ref, w1_ref, w2_ref, gfpost_ref, o_ref) = refs[2 * n:]
    tm, D = h_ref.shape
    hd = D // X_HEADS
    dff = w1_ref.shape[1]

    groups = [slice(i * tm // TAIL_SPLIT, (i + 1) * tm // TAIL_SPLIT) for i in range(TAIL_SPLIT)]
    state = [dict() for _ in groups]

    def proj(st, rows):
        acc = _mm(ys[0][rows, :], ws[0][...])
        for y, w in zip(ys[1:], ws[1:]):
            acc = acc + _mm(y[rows, :], w[...])
        st["acc"] = acc

    def norm_mix(st, rows):
        st["h"] = h_ref[rows, :] + _rms(st["acc"], gmix_ref[...])
        st["u"] = _rms(st["h"], gxpre_ref[...]).astype(MXU_DTYPE)

    def q_proj(st, rows):
        st["q"] = (_mm(st["u"], wq_ref[...]) * (hd ** -0.5 * LOG2E)).astype(MXU_DTYPE)

    def attend(st, rows):
        q = st["q"]

        def scores(hh):
            return _mm_nt(q[:, hh * hd:(hh + 1) * hd], kv_ref[:, hh * hd:(hh + 1) * hd])

        outs = []
        s_next = scores(0)
        for hh in range(X_HEADS):
            s = s_next
            if hh + 1 < X_HEADS:
                s_next = scores(hh + 1)
            p = jnp.exp2(s - jnp.max(s, axis=-1, keepdims=True))
            p = p / jnp.sum(p, axis=-1, keepdims=True)
            outs.append(_mm(p, kv_ref[:, D + hh * hd:D + (hh + 1) * hd]).astype(MXU_DTYPE))
        st["o"] = jnp.concatenate(outs, axis=1)

    def o_proj(st, rows):
        st["acc"] = _mm(st["o"], wo_ref[...])

    def norm_x(st, rows):
        st["h"] = st["h"] + _rms(st["acc"], gxpost_ref[...])
        st["u"] = _rms(st["h"], gfpre_ref[...]).astype(MXU_DTYPE)
        st["acc"] = jnp.zeros(st["h"].shape, F32)

    def mlp_chunk(c):
        sl = slice(c * FF_CHUNK, (c + 1) * FF_CHUNK)

        def stage(st, rows):
            a = jnp.maximum(jnp.dot(st["u"], w1_ref[:, sl], preferred_element_type=F32), 0.0)
            st["acc"] = st["acc"] + _mm(a * a, w2_ref[sl, :])
        return stage

    def norm_out(st, rows):
        o_ref[rows, :] = st["h"] + _rms(st["acc"], gfpost_ref[...])

    stages = ([proj, norm_mix, q_proj, attend, o_proj, norm_x]
              + [mlp_chunk(c) for c in range(dff // FF_CHUNK)] + [norm_out])
    for stage in stages:
        for st, rows in zip(state, groups):
            stage(st, rows)


def _tail(ys, ws, h, gmix, gxpre, wq, kv, wo, gxpost, gfpre, w1, w2, gfpost, B, S):
    T, D = h.shape
    M = kv.shape[0] // B
    tm = ROW_TILE
    nt = S // tm
    n = len(ys)

    def rows(width):
        return pl.BlockSpec((tm, width), lambda b, j: (b * nt + j, 0))

    g_spec = _const_spec((1, D))
    return pl.pallas_call(
        functools.partial(_tail_kernel, n=n),
        grid=(B, nt),
        in_specs=([rows(y.shape[1]) for y in ys] + [_const_spec(w.shape) for w in ws]
                  + [rows(D), g_spec, g_spec, _const_spec(wq.shape),
                     pl.BlockSpec((M, 2 * D), lambda b, j: (b, 0)),
                     _const_spec(wo.shape), g_spec, g_spec, _const_spec(w1.shape), _const_spec(w2.shape), g_spec]),
        out_specs=rows(D),
        out_shape=jax.ShapeDtypeStruct((T, D), F32),
        compiler_params=_params("parallel", "parallel"),
        name="tail",
    )(*ys, *ws, h, gmix, gxpre, wq, kv, wo, gxpost, gfpre, w1, w2, gfpost)


def _norm_mm_kernel(x_ref, g_ref, w_ref, o_ref):
    o_ref[...] = _mm(_rms(x_ref[...], g_ref[...]), w_ref[...]).astype(o_ref.dtype)


def _norm_mm(x, g, w, tm):
    T, D = x.shape
    N = w.shape[1]
    return pl.pallas_call(
        _norm_mm_kernel,
        grid=(T // tm,),
        in_specs=[pl.BlockSpec((tm, D), lambda i: (i, 0)), _const_spec((1, D)), _const_spec(w.shape)],
        out_specs=pl.BlockSpec((tm, N), lambda i: (i, 0)),
        out_shape=jax.ShapeDtypeStruct((T, N), MXU_DTYPE),
        compiler_params=_params("parallel"),
        name="norm_mm",
    )(x, g, w)


def _rope(t, cos, sin_lo, sin_hi):
    n = t.shape[1]
    half = SW_HD // 2
    return t * cos + pltpu.roll(t, n - half, 1) * sin_lo + pltpu.roll(t, half, 1) * sin_hi


def _head_pair_split(t):
    tm, n = t.shape
    lo_mask = lax.broadcasted_iota(jnp.int32, (tm, LANES), 1) < SW_HD
    los, his = [], []
    for s in range(n // LANES):
        slab = t[:, s * LANES:(s + 1) * LANES]
        swapped = pltpu.roll(slab, SW_HD, 1)
        los += [jnp.where(lo_mask, slab, 0.0), jnp.where(lo_mask, swapped, 0.0)]
        his += [jnp.where(lo_mask, 0.0, swapped), jnp.where(lo_mask, 0.0, slab)]
    return jnp.concatenate(los, axis=1), jnp.concatenate(his, axis=1)


def _qkv_rope_kernel(h_ref, g_ref, w_ref, pos_ref, inv_ref, q_ref, klo_ref, khi_ref, vlo_ref, vhi_ref):
    tm = h_ref.shape[0]
    nq = q_ref.shape[1]
    nk = SW_KV_HEADS * SW_HD
    qkv = _mm(_rms(h_ref[...], g_ref[...]), w_ref[...])
    ang = pos_ref[...].astype(F32) * inv_ref[...]
    cos = jnp.cos(ang)
    sin = jnp.sin(ang)
    first_half = (lax.broadcasted_iota(jnp.int32, (tm, LANES), 1) & (SW_HD - 1)) < SW_HD // 2
    sin_lo = jnp.where(first_half, -sin, 0.0)
    sin_hi = jnp.where(first_half, 0.0, sin)

    def tiled(x, n):
        return jnp.concatenate([x] * (n // LANES), axis=1)

    q = _rope(qkv[:, :nq], tiled(cos, nq), tiled(sin_lo, nq), tiled(sin_hi, nq))
    q_ref[...] = (q * (SW_HD ** -0.5 * LOG2E)).astype(q_ref.dtype)
    k = _rope(qkv[:, nq:nq + nk], tiled(cos, nk), tiled(sin_lo, nk), tiled(sin_hi, nk))
    klo, khi = _head_pair_split(k)
    klo_ref[...] = klo.astype(klo_ref.dtype)
    khi_ref[...] = khi.astype(khi_ref.dtype)
    vlo, vhi = _head_pair_split(qkv[:, nq + nk:])
    vlo_ref[...] = vlo.astype(vlo_ref.dtype)
    vhi_ref[...] = vhi.astype(vhi_ref.dtype)


def _qkv_rope(h, g, w, pos, inv):
    T, D = h.shape
    tm = ROW_TILE
    nq = SW_HEADS * SW_HD
    nkp = SW_KV_HEADS * LANES
    kv_spec = pl.BlockSpec((tm, nkp), lambda i: (i, 0))
    kv_shape = jax.ShapeDtypeStruct((T, nkp), MXU_DTYPE)
    return pl.pallas_call(
        _qkv_rope_kernel,
        grid=(T // tm,),
        in_specs=[pl.BlockSpec((tm, D), lambda i: (i, 0)), _const_spec((1, D)), _const_spec(w.shape),
                  pl.BlockSpec((tm, 1), lambda i: (i, 0)), _const_spec((1, LANES))],
        out_specs=[pl.BlockSpec((tm, nq), lambda i: (i, 0)), kv_spec, kv_spec, kv_spec, kv_spec],
        out_shape=[jax.ShapeDtypeStruct((T, nq), MXU_DTYPE), kv_shape, kv_shape, kv_shape, kv_shape],
        compiler_params=_params("parallel"),
        name="qkv_rope",
    )(h, g, w, pos, inv)


def _swa_kernel(sink_ref, q_ref, klp_ref, klc_ref, khp_ref, khc_ref, vlp_ref, vlc_ref, vhp_ref, vhc_ref, o_ref):
    Lb = SW_BLOCK
    nblk = q_ref.shape[0] // Lb
    kr = lax.broadcasted_iota(jnp.int32, (4 * Lb, LANES), 0)
    kc = lax.broadcasted_iota(jnp.int32, (4 * Lb, LANES), 1)
    key = kr & (2 * Lb - 1)
    diff = kc + Lb - key
    band = (diff >= 0) & (diff < Lb)
    first_lim = jnp.where(pl.program_id(1) == 0, Lb, 0)
    bias_t = jnp.where(band, 0.0, NEG_BIG)
    bias_t_first = jnp.where(key >= first_lim, bias_t, NEG_BIG)
    bias_t = bias_t.astype(MXU_DTYPE)
    bias_t_first = bias_t_first.astype(MXU_DTYPE)
    qr = lax.broadcasted_iota(jnp.int32, (2 * Lb, LANES), 0)
    qc = lax.broadcasted_iota(jnp.int32, (2 * Lb, LANES), 1)
    row_onehot = ((qr & (Lb - 1)) == qc).astype(MXU_DTYPE)
    ones_ext = ((kc >= SW_HD) == (kr >= 2 * Lb)).astype(MXU_DTYPE)
    sink_col = lax.broadcasted_iota(jnp.int32, (Lb, LANES), 1) == 0
    sink_row = lax.broadcasted_iota(jnp.int32, (2 * SUBLANES, LANES), 0) == 0

    def window(p_ref, c_ref, blk, cols):
        if blk == 0:
            return jnp.concatenate([p_ref[:, cols], c_ref[0:Lb, cols]], axis=0)
        return c_ref[(blk - 1) * Lb:(blk + 1) * Lb, cols]

    def value_window(p_ref, c_ref, blk, cols):
        w = window(p_ref, c_ref, blk, cols)
        top = jnp.where(sink_row, jnp.zeros_like(w[0:2 * SUBLANES]), w[0:2 * SUBLANES])
        return jnp.concatenate([top, w[2 * SUBLANES:]], axis=0)

    def scores(blk, kk):
        cols = slice(kk * LANES, (kk + 1) * LANES)
        rows = slice(blk * Lb, (blk + 1) * Lb)
        kbd = jnp.concatenate([window(klp_ref, klc_ref, blk, cols), window(khp_ref, khc_ref, blk, cols)], axis=0)
        kbd = jnp.concatenate([kbd, bias_t_first if blk == 0 else bias_t], axis=1)
        q2 = jnp.concatenate([q_ref[rows, (2 * kk + a) * LANES:(2 * kk + a + 1) * LANES] for a in range(2)], axis=0)
        q2 = jnp.concatenate([q2, row_onehot], axis=1)
        return _mm_nt(q2, kbd)

    def finish(blk, kk, s):
        cols = slice(kk * LANES, (kk + 1) * LANES)
        rows = slice(blk * Lb, (blk + 1) * Lb)
        vbd = jnp.concatenate([value_window(vlp_ref, vlc_ref, blk, cols), value_window(vhp_ref, vhc_ref, blk, cols)],
                              axis=0)
        vbd = jnp.concatenate([vbd, ones_ext], axis=1)
        p_rows = []
        for a in range(2):
            p_cols = []
            for hb in range(2):
                sink = sink_ref[SW_GROUP * kk + 2 * a + hb] * LOG2E
                c0 = hb * 2 * Lb
                sq = jnp.concatenate([jnp.where(sink_col, sink, s[a * Lb:(a + 1) * Lb, c0:c0 + LANES]),
                                      s[a * Lb:(a + 1) * Lb, c0 + LANES:c0 + 2 * Lb]], axis=1)
                m = jnp.max(sq, axis=-1, keepdims=True)
                p_cols.append(jnp.exp2(sq - m).astype(MXU_DTYPE))
            p_rows.append(jnp.concatenate(p_cols, axis=1))
        ov = _mm(jnp.concatenate(p_rows, axis=0), vbd)
        for a in range(2):
            oa = ov[a * Lb:(a + 1) * Lb]
            o_ref[rows, (2 * kk + a) * LANES:(2 * kk + a + 1) * LANES] = (
                oa[:, :LANES] / oa[:, LANES:]).astype(o_ref.dtype)

    units = [(blk, kk) for blk in range(nblk) for kk in range(SW_KV_HEADS)]
    s_next = scores(*units[0])
    for i, unit in enumerate(units):
        s_cur = s_next
        if i + 1 < len(units):
            s_next = scores(*units[i + 1])
        finish(*unit, s_cur)


def _swa(q, klo, khi, vlo, vhi, sinks, B, S):
    T, nq = q.shape
    Lb = SW_BLOCK
    tq = SW_TQ
    nt = S // tq
    per = tq // Lb
    nkp = SW_KV_HEADS * LANES

    def cur(b, j):
        return (b * nt + j, 0)

    def prev(b, j):
        return (b * nt * per + jnp.maximum(j * per - 1, 0), 0)

    p_spec = pl.BlockSpec((Lb, nkp), prev)
    c_spec = pl.BlockSpec((tq, nkp), cur)
    return pl.pallas_call(
        _swa_kernel,
        grid=(B, nt),
        in_specs=[pl.BlockSpec(memory_space=pltpu.SMEM), pl.BlockSpec((tq, nq), cur),
                  p_spec, c_spec, p_spec, c_spec, p_spec, c_spec, p_spec, c_spec],
        out_specs=pl.BlockSpec((tq, nq), cur),
        out_shape=jax.ShapeDtypeStruct((T, nq), MXU_DTYPE),
        compiler_params=_params("parallel", "parallel"),
        name="swa",
    )(sinks, q, klo, klo, khi, khi, vlo, vlo, vhi, vhi)


def kernel(x, mem, positions, ev_w_in, ev_b_if, ev_conv_w, ev_conv_b, ev_rg_wa, ev_rg_ba, ev_rg_wx, ev_rg_bx,
           ev_rg_lambda, ev_w_out, od_w_qkv, od_sinks, od_w_o, g_mix_pre, g_mix_post, g_x_pre, g_x_post, g_mem,
           w_xq, w_xkv, w_xo, g_ff_pre, g_ff_post, w_ff1, w_ff2):
    B, S, D = x.shape
    T = B * S
    M = mem.shape[1]
    depth = g_mix_pre.shape[0]
    bf = MXU_DTYPE
    mlw = ML_HEADS * ML_HD
    rgw = RG_BLOCKS * RG_BW
    ng = 2 * ML_HEADS

    h = x.reshape(T, D)
    mem2 = mem.reshape(B * M, D)
    pos = positions.reshape(T, 1)
    half = SW_HD // 2
    inv = jnp.power(ROPE_THETA, -jnp.arange(half, dtype=F32) * 2.0 / SW_HD)
    inv = jnp.tile(inv, LANES // half)[None, :]

    def row(v):
        return v.reshape(1, -1)

    for l in range(depth):
        if l % 2 == 0:
            e = l // 2
            w_in = ev_w_in[e]
            wa = w_in[:, :4 * mlw].astype(bf)
            wg32 = w_in[:, 4 * mlw:4 * mlw + ng]
            wb = w_in[:, 4 * mlw + ng:].astype(bf)
            wgt = jnp.pad(wg32.T, ((0, GATE_ROWS - ng), (0, 0))).astype(bf)
            br = jnp.pad(ev_b_if[e], (0, GATE_ROWS - ng)).reshape(GATE_ROWS, 1)
            za, gr, yb = _inproj(h, row(g_mix_pre[l]), wa, wb, wgt, br,
                                 ev_conv_w[e], row(ev_conv_b[e]), ev_rg_wa[e].astype(bf), row(ev_rg_ba[e]),
                                 ev_rg_wx[e].astype(bf), row(ev_rg_bx[e]), row(ev_rg_lambda[e]), B, S)
            ya = _mlstm(za, gr, B, S)
            w_out = ev_w_out[e].astype(bf)
            ys, ws = [ya, yb], [w_out[:mlw], w_out[mlw:]]
        else:
            o = l // 2
            q, klo, khi, vlo, vhi = _qkv_rope(h, row(g_mix_pre[l]), od_w_qkv[o].astype(bf), pos, inv)
            ys, ws = [_swa(q, klo, khi, vlo, vhi, od_sinks[o], B, S)], [od_w_o[o].astype(bf)]
        kv = _norm_mm(mem2, row(g_mem[l]), w_xkv[l].astype(bf), M)
        h = _tail(ys, ws, h, row(g_mix_post[l]), row(g_x_pre[l]), w_xq[l].astype(bf), kv, w_xo[l].astype(bf),
                  row(g_x_post[l]), row(g_ff_pre[l]), w_ff1[l].astype(bf), w_ff2[l].astype(bf), row(g_ff_post[l]),
                  B, S)
    return h.reshape(B, S, D)
```
